```python
import jax, jax.numpy as jnp
from jax import lax
import numpy as np

D_MODEL = 1024
BATCH = 8
SEQ = 4096
DEPTH = 1

GRID_W = 64
EPS = 1e-6
NEG_INF = -1e30

NA_HEADS = 8
NA_HEAD_DIM = 64
NA_WIN_H = 8
NA_WIN_W = 16
NA_QBLOCK_W = 16
NA_BAND_W = NA_QBLOCK_W + NA_WIN_W
NA_WIDTH = NA_HEADS * NA_HEAD_DIM

MLA_HEADS = 8
MLA_QK_NOPE = 64
MLA_QK_ROPE = 32
MLA_V_DIM = 64
MLA_Q_LORA = 256
MLA_KV_LORA = 128
MLA_QBLOCK = 128
MLA_WIDTH = MLA_HEADS * MLA_V_DIM
ROPE_THETA = 10000.0

MIX_WIDTH = NA_WIDTH + MLA_WIDTH
D_IN = 3 * NA_WIDTH + MLA_Q_LORA + MLA_KV_LORA + MLA_QK_ROPE
D_FF = 2816

kernel_name = "hybrid_na_mla_macaron_block"


def rms_norm(x, g):
    xf = x.astype(jnp.float32)
    y = xf * lax.rsqrt(jnp.mean(xf * xf, axis=-1, keepdims=True) + EPS)
    return (y * g.astype(jnp.float32)).astype(x.dtype)


def swiglu(h, w_gu, w_down):
    gate, up = jnp.split(h @ w_gu, 2, axis=-1)
    return (jax.nn.silu(gate) * up) @ w_down


def grid_rope_tables(seq):
    t = jnp.arange(seq)
    row = (t // GRID_W).astype(jnp.float32)
    col = (t % GRID_W).astype(jnp.float32)
    n_freq = MLA_QK_ROPE // 4
    inv_freq = 1.0 / (ROPE_THETA ** (jnp.arange(n_freq, dtype=jnp.float32) / n_freq))
    ang = jnp.concatenate([row[:, None] * inv_freq[None, :], col[:, None] * inv_freq[None, :]], axis=-1)
    return jnp.cos(ang), jnp.sin(ang)


def apply_rope(x, cos, sin):
    xf = x.astype(jnp.float32)
    x1, x2 = jnp.split(xf, 2, axis=-1)
    return jnp.concatenate([x1 * cos - x2 * sin, x2 * cos + x1 * sin], axis=-1).astype(x.dtype)


def neighbourhood_attention(q, k, v, rpb):
    B, S, H, D = q.shape
    R = S // GRID_W
    KH = min(NA_WIN_H, R)
    NCB = GRID_W // NA_QBLOCK_W
    rows = jnp.arange(R)
    row_start = jnp.clip(rows - KH // 2, 0, R - KH)
    key_rows = row_start[:, None] + jnp.arange(KH)
    c0 = jnp.arange(NCB) * NA_QBLOCK_W
    band_start = jnp.clip(c0 - NA_WIN_W // 2, 0, GRID_W - NA_BAND_W)
    key_cols = band_start[:, None] + jnp.arange(NA_BAND_W)
    q_cols = c0[:, None] + jnp.arange(NA_QBLOCK_W)
    win_start = jnp.clip(q_cols - NA_WIN_W // 2, 0, GRID_W - NA_WIN_W)
    kc = key_cols[:, None, :]
    col_mask = (kc >= win_start[..., None]) & (kc < win_start[..., None] + NA_WIN_W)
    row_off = key_rows - rows[:, None] + (NA_WIN_H - 1)
    col_off = jnp.clip(kc - q_cols[:, :, None] + (NA_WIN_W - 1), 0, 2 * NA_WIN_W - 2)
    bias = rpb.astype(jnp.float32)[:, row_off[:, None, None, :, None], col_off[None, :, :, None, :]]
    bias = bias.transpose(1, 2, 0, 3, 4, 5)
    bias = jnp.where(col_mask[:, None, :, None, :], bias, NEG_INF)
    qg = q.reshape(B, R, NCB, NA_QBLOCK_W, H, D)
    ridx = key_rows[:, None, :, None]
    cidx = key_cols[None, :, None, :]
    kg = k.reshape(B, R, GRID_W, H, D)[:, ridx, cidx]
    vg = v.reshape(B, R, GRID_W, H, D)[:, ridx, cidx]
    s = jnp.einsum('brnqhd,brnijhd->brnhqij', qg, kg).astype(jnp.float32) * (D ** -0.5)
    s = s + bias[None]
    sh = s.shape
    p = jax.nn.softmax(s.reshape(sh[:-2] + (sh[-2] * sh[-1],)), axis=-1).reshape(sh)
    o = jnp.einsum('brnhqij,brnijhd->brnqhd', p.astype(v.dtype), vg)
    return o.reshape(B, S, H * D)


def latent_attention(c_q, c_kv, k_rope, q_norm_g, w_uq, kv_norm_g, w_ukv, cos, sin):
    B, S, _ = c_q.shape
    H = MLA_HEADS
    q = (rms_norm(c_q, q_norm_g) @ w_uq).reshape(B, S, H, MLA_QK_NOPE + MLA_QK_ROPE)
    q_nope, q_rot = jnp.split(q, [MLA_QK_NOPE], axis=-1)
    kv = (rms_norm(c_kv, kv_norm_g) @ w_ukv).reshape(B, S, H, MLA_QK_NOPE + MLA_V_DIM)
    k_nope, v = jnp.split(kv, [MLA_QK_NOPE], axis=-1)
    q_rot = apply_rope(q_rot, cos[:, None, :], sin[:, None, :])
    k_rot = apply_rope(k_rope, cos, sin)
    scale = (MLA_QK_NOPE + MLA_QK_ROPE) ** -0.5
    nqb = S // MLA_QBLOCK

    def block(args):
        qn, qr = args
        s = (jnp.einsum('bqhd,bkhd->bhqk', qn, k_nope)
             + jnp.einsum('bqhd,bkd->bhqk', qr, k_rot)).astype(jnp.float32) * scale
        p = jax.nn.softmax(s, axis=-1)
        return jnp.einsum('bhqk,bkhd->bqhd', p.astype(v.dtype), v)

    qn_b = q_nope.reshape(B, nqb, MLA_QBLOCK, H, MLA_QK_NOPE).swapaxes(0, 1)
    qr_b = q_rot.reshape(B, nqb, MLA_QBLOCK, H, MLA_QK_ROPE).swapaxes(0, 1)
    o = lax.map(block, (qn_b, qr_b))
    return o.swapaxes(0, 1).reshape(B, S, H * MLA_V_DIM)


def setup_inputs(seed: int = 0) -> dict:
    key = jax.random.key(seed)
    ks = iter(jax.random.split(key, 32))

    def w(shape, fan_in):
        return jax.random.normal(next(ks), shape, jnp.float32) * (fan_in ** -0.5)

    def gain(n):
        return 1.0 + 0.05 * jax.random.normal(next(ks), (DEPTH, n), jnp.float32)

    L = DEPTH
    return {
        "x": jax.random.normal(next(ks), (BATCH, SEQ, D_MODEL), jnp.float32),
        "ffn1_pre_g": gain(D_MODEL),
        "ffn1_w_gu": w((L, D_MODEL, 2 * D_FF), D_MODEL),
        "ffn1_w_down": w((L, D_FF, D_MODEL), D_FF),
        "ffn1_post_g": gain(D_MODEL),
        "mix_pre_g": gain(D_MODEL),
        "w_in": w((L, D_MODEL, D_IN), D_MODEL),
        "na_rpb": 0.1 * jax.random.normal(next(ks), (L, NA_HEADS, 2 * NA_WIN_H - 1, 2 * NA_WIN_W - 1), jnp.float32),
        "mla_q_norm_g": gain(MLA_Q_LORA),
        "mla_w_uq": w((L, MLA_Q_LORA, MLA_HEADS * (MLA_QK_NOPE + MLA_QK_ROPE)), MLA_Q_LORA),
        "mla_kv_norm_g": gain(MLA_KV_LORA),
        "mla_w_ukv": w((L, MLA_KV_LORA, MLA_HEADS * (MLA_QK_NOPE + MLA_V_DIM)), MLA_KV_LORA),
        "na_out_norm_g": gain(NA_WIDTH),
        "mla_out_norm_g": gain(MLA_WIDTH),
        "w_out": w((L, MIX_WIDTH, D_MODEL), MIX_WIDTH),
        "mix_post_g": gain(D_MODEL),
        "ffn2_pre_g": gain(D_MODEL),
        "ffn2_w_gu": w((L, D_MODEL, 2 * D_FF), D_MODEL),
        "ffn2_w_down": w((L, D_FF, D_MODEL), D_FF),
        "ffn2_post_g": gain(D_MODEL),
    }


def reference(x, ffn1_pre_g, ffn1_w_gu, ffn1_w_down, ffn1_post_g, mix_pre_g, w_in, na_rpb,
              mla_q_norm_g, mla_w_uq, mla_kv_norm_g, mla_w_ukv, na_out_norm_g, mla_out_norm_g,
              w_out, mix_post_g, ffn2_pre_g, ffn2_w_gu, ffn2_w_down, ffn2_post_g):
    B, S, _ = x.shape
    cos, sin = grid_rope_tables(S)
    split_at = [NA_WIDTH, 2 * NA_WIDTH, 3 * NA_WIDTH,
                3 * NA_WIDTH + MLA_Q_LORA, 3 * NA_WIDTH + MLA_Q_LORA + MLA_KV_LORA]
    h = x
    for l in range(DEPTH):
        f = swiglu(rms_norm(h, ffn1_pre_g[l]), ffn1_w_gu[l], ffn1_w_down[l])
        h = h + 0.5 * rms_norm(f, ffn1_post_g[l])
        z = rms_norm(h, mix_pre_g[l]) @ w_in[l]
        q_na, k_na, v_na, c_q, c_kv, k_rope = jnp.split(z, split_at, axis=-1)
        hs = (B, S, NA_HEADS, NA_HEAD_DIM)
        o_na = neighbourhood_attention(q_na.reshape(hs), k_na.reshape(hs), v_na.reshape(hs), na_rpb[l])
        o_mla = latent_attention(c_q, c_kv, k_rope, mla_q_norm_g[l], mla_w_uq[l],
                                 mla_kv_norm_g[l], mla_w_ukv[l], cos, sin)
        mixed = jnp.concatenate([rms_norm(o_na, na_out_norm_g[l]),
                                 rms_norm(o_mla, mla_out_norm_g[l])], axis=-1) @ w_out[l]
        h = h + rms_norm(mixed, mix_post_g[l])
        f = swiglu(rms_norm(h, ffn2_pre_g[l]), ffn2_w_gu[l], ffn2_w_down[l])
        h = h + 0.5 * rms_norm(f, ffn2_post_g[l])
    return h
```

```python
import functools

import numpy as np
import jax
import jax.numpy as jnp
from jax import lax
from jax.experimental import pallas as pl
from jax.experimental.pallas import tpu as pltpu

GRID_W = 64
EPS = 1e-6
NEG_INF = -1e30

NA_HEADS = 8
NA_HEAD_DIM = 64
NA_WIN_H = 8
NA_WIN_W = 16
NA_WIDTH = NA_HEADS * NA_HEAD_DIM

MLA_HEADS = 8
MLA_QK_NOPE = 64
MLA_QK_ROPE = 32
MLA_V_DIM = 64
MLA_Q_LORA = 256
MLA_KV_LORA = 128
MLA_WIDTH = MLA_HEADS * MLA_V_DIM
ROPE_THETA = 10000.0

LANES = 128
HEAD_SLOT = LANES
ROT_LANE0 = MLA_QK_NOPE
HALF_ROT = MLA_QK_ROPE // 2

NA_GROUP_ROWS = 4
NA_KEY_ROWS = NA_GROUP_ROWS + NA_WIN_H - 1

VMEM_LIMIT = 56 * 1024 * 1024

_NT = (((1,), (1,)), ((), ()))


def _rms(x, g):
    return x * lax.rsqrt(jnp.mean(x * x, axis=-1, keepdims=True) + EPS) * g


def _bdot(a, b):
    return jnp.dot(a, b, preferred_element_type=jnp.float32)


def _ffn_kernel(x_ref, pre_g_ref, wg_ref, wu_ref, wd_ref, post_g_ref, o_ref, *, ff_chunk):
    x = x_ref[...]
    xn = _rms(x, pre_g_ref[...]).astype(jnp.bfloat16)
    d_ff = wg_ref.shape[1]
    acc = jnp.zeros(x.shape, jnp.float32)
    for c0 in range(0, d_ff, ff_chunk):
        gate = _bdot(xn, wg_ref[:, c0:c0 + ff_chunk])
        up = _bdot(xn, wu_ref[:, c0:c0 + ff_chunk])
        act = (gate * jax.nn.sigmoid(gate) * up).astype(jnp.bfloat16)
        acc = acc + _bdot(act, wd_ref[c0:c0 + ff_chunk, :])
    o_ref[...] = x + 0.5 * _rms(acc, post_g_ref[...])


def _ffn(x, pre_g, w_gate, w_up, w_down, post_g, *, tm=512, ff_chunk=1408):
    n, d = x.shape
    d_ff = w_gate.shape[1]
    assert n % tm == 0 and d_ff % ff_chunk == 0
    row = pl.BlockSpec((tm, d), lambda i: (i, 0))
    full = lambda a: pl.BlockSpec(a.shape, lambda i: (0,) * a.ndim)
    return pl.pallas_call(
        functools.partial(_ffn_kernel, ff_chunk=ff_chunk),
        grid=(n // tm,),
        in_specs=[row, full(pre_g), full(w_gate), full(w_up), full(w_down), full(post_g)],
        out_specs=row,
        out_shape=jax.ShapeDtypeStruct((n, d), jnp.float32),
        compiler_params=pltpu.CompilerParams(
            dimension_semantics=("arbitrary",), vmem_limit_bytes=VMEM_LIMIT),
        name="ffn",
    )(x, pre_g, w_gate, w_up, w_down, post_g)


def _rope(x, cos_t, sin_lo, sin_hi):
    from_hi = pltpu.roll(x, LANES - HALF_ROT, axis=1)
    from_lo = pltpu.roll(x, HALF_ROT, axis=1)
    return x * cos_t + from_hi * sin_lo + from_lo * sin_hi


def _proj_kernel(h_ref, pre_g_ref, w_in_ref, qn_g_ref, w_uq_ref, kvn_g_ref, w_ukv_ref,
                 cos_ref, sin_lo_ref, sin_hi_ref,
                 qna_ref, kna_ref, vna_ref, qm_ref, kr_ref, kv_ref, *, na_scale, mla_scale):
    hn = _rms(h_ref[...], pre_g_ref[...]).astype(jnp.bfloat16)
    z = _bdot(hn, w_in_ref[...])
    w = NA_WIDTH
    qna_ref[...] = (z[:, 0:w] * na_scale).astype(jnp.bfloat16)
    kna_ref[...] = z[:, w:2 * w].astype(jnp.bfloat16)
    vna_ref[...] = z[:, 2 * w:3 * w].astype(jnp.bfloat16)
    o = 3 * w
    c_q = z[:, o:o + MLA_Q_LORA]
    c_kv = z[:, o + MLA_Q_LORA:o + MLA_Q_LORA + MLA_KV_LORA]
    k_rope = z[:, o + MLA_Q_LORA + MLA_KV_LORA:]

    cos_t, sin_lo, sin_hi = cos_ref[...], sin_lo_ref[...], sin_hi_ref[...]
    q = _bdot(_rms(c_q, qn_g_ref[...]).astype(jnp.bfloat16), w_uq_ref[...])
    kv = _bdot(_rms(c_kv, kvn_g_ref[...]).astype(jnp.bfloat16), w_ukv_ref[...])
    k_rot = _rope(k_rope, cos_t, sin_lo, sin_hi)
    kv_ref[...] = kv.astype(jnp.bfloat16)
    lane = lax.broadcasted_iota(jnp.int32, k_rot.shape, 1)
    for h in range(MLA_HEADS):
        sl = slice(h * HEAD_SLOT, (h + 1) * HEAD_SLOT)
        qm_ref[:, sl] = (_rope(q[:, sl], cos_t, sin_lo, sin_hi) * mla_scale).astype(jnp.bfloat16)
        kr_ref[:, sl] = jnp.where(lane < MLA_QK_NOPE, kv[:, sl], k_rot).astype(jnp.bfloat16)


def _proj(h, pre_g, w_in, qn_g, w_uq, kvn_g, w_ukv, cos_t, sin_lo, sin_hi, *, seq, tm=512):
    n, d = h.shape
    assert n % tm == 0 and seq % tm == 0
    tiles_per_seq = seq // tm
    row = lambda width: pl.BlockSpec((tm, width), lambda i: (i, 0))
    full = lambda a: pl.BlockSpec(a.shape, lambda i: (0,) * a.ndim)
    tab = pl.BlockSpec((tm, LANES), lambda i: (i % tiles_per_seq, 0))
    slots = MLA_HEADS * HEAD_SLOT
    bf = jnp.bfloat16
    return pl.pallas_call(
        functools.partial(_proj_kernel, na_scale=NA_HEAD_DIM ** -0.5,
                          mla_scale=(MLA_QK_NOPE + MLA_QK_ROPE) ** -0.5),
        grid=(n // tm,),
        in_specs=[row(d), full(pre_g), full(w_in), full(qn_g), full(w_uq), full(kvn_g),
                  full(w_ukv), tab, tab, tab],
        out_specs=[row(NA_WIDTH), row(NA_WIDTH), row(NA_WIDTH), row(slots), row(slots), row(slots)],
        out_shape=[jax.ShapeDtypeStruct((n, NA_WIDTH), bf)] * 3
        + [jax.ShapeDtypeStruct((n, slots), bf)] * 3,
        compiler_params=pltpu.CompilerParams(
            dimension_semantics=("arbitrary",), vmem_limit_bytes=VMEM_LIMIT),
        name="proj",
    )(h, pre_g, w_in, qn_g, w_uq, kvn_g, w_ukv, cos_t, sin_lo, sin_hi)


def _softmax_pv(s, v):
    m = jnp.max(s, axis=-1, keepdims=True)
    p = jnp.exp(s - m)
    l = jnp.sum(p, axis=-1, keepdims=True)
    return _bdot(p.astype(jnp.bfloat16), v) * (1.0 / l)


def _na_kernel(q_ref, k_ref, v_ref, bias_ref, o_ref, *, n_rows):
    g = pl.program_id(1)
    key_row0 = jnp.clip(g * NA_GROUP_ROWS - NA_WIN_H // 2, 0, n_rows - NA_KEY_ROWS)
    start = pl.multiple_of(key_row0 * GRID_W, GRID_W)
    n_keys = NA_KEY_ROWS * GRID_W
    lane = lax.broadcasted_iota(jnp.int32, (NA_GROUP_ROWS * GRID_W, LANES), 1)
    low = lane < NA_HEAD_DIM
    for pair in range(NA_HEADS // 2):
        sl = slice(pair * LANES, (pair + 1) * LANES)
        q = q_ref[0, :, sl]
        k = k_ref[0, pl.ds(start, n_keys), sl]
        v = v_ref[0, pl.ds(start, n_keys), sl]
        outs = []
        for e in range(2):
            qe = jnp.where(low if e == 0 else ~low, q, jnp.zeros_like(q))
            s = lax.dot_general(qe, k, _NT, preferred_element_type=jnp.float32)
            outs.append(_softmax_pv(s + bias_ref[0, 2 * pair + e], v))
        o_ref[0, :, sl] = jnp.where(low, outs[0], outs[1]).astype(o_ref.dtype)


def _na_bias(rpb, n_rows):
    G, KR, W = NA_GROUP_ROWS, NA_KEY_ROWS, GRID_W
    n_groups = n_rows // G
    kh = min(NA_WIN_H, n_rows)
    qc = np.arange(W)[:, None]
    kc = np.arange(W)[None, :]
    win0 = np.clip(qc - NA_WIN_W // 2, 0, W - NA_WIN_W)
    col_ok = (kc >= win0) & (kc < win0 + NA_WIN_W)
    col_off = np.clip(kc - qc + NA_WIN_W - 1, 0, 2 * NA_WIN_W - 2)
    variants = []
    for g in (0, 1, n_groups - 1):
        r = g * G + np.arange(G)[:, None]
        key_row0 = np.clip(g * G - NA_WIN_H // 2, 0, n_rows - KR)
        kr = key_row0 + np.arange(KR)[None, :]
        row0 = np.clip(r - kh // 2, 0, n_rows - kh)
        row_ok = (kr >= row0) & (kr < row0 + kh)
        row_off = np.clip(kr - r + NA_WIN_H - 1, 0, 2 * NA_WIN_H - 2)
        ok = row_ok[:, None, :, None] & col_ok[None, :, None, :]
        b = rpb[:, row_off[:, None, :, None], col_off[None, :, None, :]]
        b = jnp.where(ok[None], b, NEG_INF)
        variants.append(b.reshape(rpb.shape[0], G * W, KR * W))
    return jnp.stack(variants)


def _na_attention(q, k, v, bias):
    b, s, w = q.shape
    n_rows = s // GRID_W
    assert n_rows % NA_GROUP_ROWS == 0 and n_rows >= NA_KEY_ROWS and n_rows // NA_GROUP_ROWS >= 3
    n_groups = n_rows // NA_GROUP_ROWS
    tq = NA_GROUP_ROWS * GRID_W

    def bias_map(bi, g):
        variant = jnp.where(g == 0, 0, jnp.where(g == n_groups - 1, 2, 1))
        return (variant, 0, 0, 0)

    whole = pl.BlockSpec((1, s, w), lambda bi, g: (bi, 0, 0))
    qblk = pl.BlockSpec((1, tq, w), lambda bi, g: (bi, g, 0))
    return pl.pallas_call(
        functools.partial(_na_kernel, n_rows=n_rows),
        grid=(b, n_groups),
        in_specs=[qblk, whole, whole, pl.BlockSpec((1,) + bias.shape[1:], bias_map)],
        out_specs=qblk,
        out_shape=jax.ShapeDtypeStruct((b, s, w), jnp.bfloat16),
        compiler_params=pltpu.CompilerParams(
            dimension_semantics=("arbitrary", "arbitrary"), vmem_limit_bytes=VMEM_LIMIT),
        name="na_attn",
    )(q, k, v, bias)


def _mla_kernel(q_ref, kr_ref, kv_ref, o_ref):
    outs = []
    for e in range(2):
        sl = slice(e * HEAD_SLOT, (e + 1) * HEAD_SLOT)
        s = lax.dot_general(q_ref[0, :, sl], kr_ref[0, :, sl], _NT,
                            preferred_element_type=jnp.float32)
        outs.append(_softmax_pv(s, kv_ref[0, :, sl]))
    lane = lax.broadcasted_iota(jnp.int32, outs[0].shape, 1)
    even = pltpu.roll(outs[0], LANES - MLA_V_DIM, axis=1)
    o_ref[0] = jnp.where(lane < MLA_V_DIM, even, outs[1]).astype(o_ref.dtype)


def _mla_attention(qm, kr, kv, *, tq=512):
    b, s, slots = qm.shape
    n_pairs = MLA_HEADS // 2
    assert s % tq == 0 and slots == MLA_HEADS * HEAD_SLOT
    pair_w = 2 * HEAD_SLOT
    qblk = pl.BlockSpec((1, tq, pair_w), lambda bi, p, t: (bi, t, p))
    keys = pl.BlockSpec((1, s, pair_w), lambda bi, p, t: (bi, 0, p))
    return pl.pallas_call(
        _mla_kernel,
        grid=(b, n_pairs, s // tq),
        in_specs=[qblk, keys, keys],
        out_specs=pl.BlockSpec((1, tq, 2 * MLA_V_DIM), lambda bi, p, t: (bi, t, p)),
        out_shape=jax.ShapeDtypeStruct((b, s, MLA_WIDTH), jnp.bfloat16),
        compiler_params=pltpu.CompilerParams(
            dimension_semantics=("arbitrary", "arbitrary", "arbitrary"),
            vmem_limit_bytes=VMEM_LIMIT),
        name="mla_attn",
    )(qm, kr, kv)


def _mix_out_kernel(h_ref, ona_ref, omla_ref, na_g_ref, mla_g_ref, w_na_ref, w_mla_ref,
                    post_g_ref, o_ref):
    na = _rms(ona_ref[...].astype(jnp.float32), na_g_ref[...]).astype(jnp.bfloat16)
    mla = _rms(omla_ref[...].astype(jnp.float32), mla_g_ref[...]).astype(jnp.bfloat16)
    mixed = _bdot(na, w_na_ref[...]) + _bdot(mla, w_mla_ref[...])
    o_ref[...] = h_ref[...] + _rms(mixed, post_g_ref[...])


def _mix_out(h, o_na, o_mla, na_g, mla_g, w_na, w_mla, post_g, *, tm=512):
    n, d = h.shape
    assert n % tm == 0
    row = lambda width: pl.BlockSpec((tm, width), lambda i: (i, 0))
    full = lambda a: pl.BlockSpec(a.shape, lambda i: (0,) * a.ndim)
    return pl.pallas_call(
        _mix_out_kernel,
        grid=(n // tm,),
        in_specs=[row(d), row(o_na.shape[1]), row(o_mla.shape[1]), full(na_g), full(mla_g),
                  full(w_na), full(w_mla), full(post_g)],
        out_specs=row(d),
        out_shape=jax.ShapeDtypeStruct((n, d), jnp.float32),
        compiler_params=pltpu.CompilerParams(
            dimension_semantics=("arbitrary",), vmem_limit_bytes=VMEM_LIMIT),
        name="mix_out",
    )(h, o_na, o_mla, na_g, mla_g, w_na, w_mla, post_g)


def _rope_tables(seq):
    t = jnp.arange(seq)
    row = (t // GRID_W).astype(jnp.float32)
    col = (t % GRID_W).astype(jnp.float32)
    n_freq = MLA_QK_ROPE // 4
    inv_freq = 1.0 / (ROPE_THETA ** (jnp.arange(n_freq, dtype=jnp.float32) / n_freq))
    ang = jnp.concatenate([row[:, None] * inv_freq[None, :], col[:, None] * inv_freq[None, :]], axis=-1)
    cos, sin = jnp.cos(ang), jnp.sin(ang)
    zeros = lambda width: jnp.zeros((seq, width), jnp.float32)
    tail = LANES - ROT_LANE0 - MLA_QK_ROPE
    cos_t = jnp.concatenate([jnp.ones((seq, ROT_LANE0), jnp.float32), cos, cos, zeros(tail)], axis=-1)
    sin_lo = jnp.concatenate([zeros(ROT_LANE0), -sin, zeros(HALF_ROT + tail)], axis=-1)
    sin_hi = jnp.concatenate([zeros(ROT_LANE0 + HALF_ROT), sin, zeros(tail)], axis=-1)
    return cos_t, sin_lo, sin_hi


def _pad_cols(w, left, right):
    return jnp.pad(w, ((0, 0), (left, right)))


def kernel(x, ffn1_pre_g, ffn1_w_gu, ffn1_w_down, ffn1_post_g, mix_pre_g, w_in, na_rpb, mla_q_norm_g, mla_w_uq, mla_kv_norm_g, mla_w_ukv, na_out_norm_g, mla_out_norm_g, w_out, mix_post_g, ffn2_pre_g, ffn2_w_gu, ffn2_w_down, ffn2_post_g):
    B, S, D = x.shape
    depth = ffn1_w_gu.shape[0]
    d_ff = ffn1_w_down.shape[1]
    bf = jnp.bfloat16
    n_rows = S // GRID_W
    cos_t, sin_lo, sin_hi = _rope_tables(S)
    row2 = lambda g: g.reshape(1, -1)

    h = x.reshape(B * S, D)
    for l in range(depth):
        h = _ffn(h, row2(ffn1_pre_g[l]), ffn1_w_gu[l, :, :d_ff].astype(bf),
                 ffn1_w_gu[l, :, d_ff:].astype(bf), ffn1_w_down[l].astype(bf), row2(ffn1_post_g[l]))

        n_main = 3 * NA_WIDTH + MLA_Q_LORA + MLA_KV_LORA
        w_in_l = jnp.concatenate(
            [w_in[l, :, :n_main],
             _pad_cols(w_in[l, :, n_main:], ROT_LANE0, LANES - ROT_LANE0 - MLA_QK_ROPE)], axis=-1).astype(bf)
        qk = MLA_QK_NOPE + MLA_QK_ROPE
        w_uq_l = jnp.pad(mla_w_uq[l].reshape(MLA_Q_LORA, MLA_HEADS, qk),
                         ((0, 0), (0, 0), (0, HEAD_SLOT - qk))).reshape(MLA_Q_LORA, MLA_HEADS * HEAD_SLOT).astype(bf)
        qna, kna, vna, qm, kr, kv = _proj(
            h, row2(mix_pre_g[l]), w_in_l, row2(mla_q_norm_g[l]), w_uq_l, row2(mla_kv_norm_g[l]),
            mla_w_ukv[l].astype(bf), cos_t, sin_lo, sin_hi, seq=S)

        bias = _na_bias(na_rpb[l].astype(jnp.float32), n_rows)
        r3 = lambda a: a.reshape(B, S, a.shape[-1])
        o_na = _na_attention(r3(qna), r3(kna), r3(vna), bias)
        o_mla = _mla_attention(r3(qm), r3(kr), r3(kv))

        h = _mix_out(h, o_na.reshape(B * S, NA_WIDTH), o_mla.reshape(B * S, MLA_WIDTH),
                     row2(na_out_norm_g[l]), row2(mla_out_norm_g[l]),
                     w_out[l, :NA_WIDTH].astype(bf), w_out[l, NA_WIDTH:].astype(bf), row2(mix_post_g[l]))

        h = _ffn(h, row2(ffn2_pre_g[l]), ffn2_w_gu[l, :, :d_ff].astype(bf),
                 ffn2_w_gu[l, :, d_ff:].astype(bf), ffn2_w_down[l].astype(bf), row2(ffn2_post_g[l]))
    return h.reshape(B, S, D)
```

```python
import functools

import numpy as np
import jax
import jax.numpy as jnp
from jax import lax
from jax.experimental import pallas as pl
from jax.experimental.pallas import tpu as pltpu

GRID_W = 64
EPS = 1e-6
NEG_INF = -1e30

NA_HEADS = 8
NA_HEAD_DIM = 64
NA_WIN_H = 8
NA_WIN_W = 16
NA_WIDTH = NA_HEADS * NA_HEAD_DIM

MLA_HEADS = 8
MLA_QK_NOPE = 64
MLA_QK_ROPE = 32
MLA_V_DIM = 64
MLA_Q_LORA = 256
MLA_KV_LORA = 128
MLA_WIDTH = MLA_HEADS * MLA_V_DIM
ROPE_THETA = 10000.0

LANES = 128
HEAD_SLOT = LANES
ROT_LANE0 = MLA_QK_NOPE
HALF_ROT = MLA_QK_ROPE // 2

NA_GROUP_ROWS = 4
NA_KEY_ROWS = NA_GROUP_ROWS + NA_WIN_H - 1

VMEM_LIMIT = 56 * 1024 * 1024

_NT = (((1,), (1,)), ((), ()))


def _rms(x, g):
    return x * lax.rsqrt(jnp.mean(x * x, axis=-1, keepdims=True) + EPS) * g


def _bdot(a, b):
    return jnp.dot(a, b, preferred_element_type=jnp.float32)


def _ffn_kernel(x_ref, pre_g_ref, wg_ref, wu_ref, wd_ref, post_g_ref, o_ref, *, ff_chunk):
    x = x_ref[...]
    xn = _rms(x, pre_g_ref[...]).astype(jnp.bfloat16)
    d_ff = wg_ref.shape[1]
    acc = jnp.zeros(x.shape, jnp.float32)
    for c0 in range(0, d_ff, ff_chunk):
        gate = _bdot(xn, wg_ref[:, c0:c0 + ff_chunk])
        up = _bdot(xn, wu_ref[:, c0:c0 + ff_chunk])
        act = (gate * jax.nn.sigmoid(gate) * up).astype(jnp.bfloat16)
        acc = acc + _bdot(act, wd_ref[c0:c0 + ff_chunk, :])
    o_ref[...] = x + 0.5 * _rms(acc, post_g_ref[...])


def _ffn(x, pre_g, w_gate, w_up, w_down, post_g, *, tm=512, ff_chunk=1408):
    n, d = x.shape
    d_ff = w_gate.shape[1]
    assert n % tm == 0 and d_ff % ff_chunk == 0
    row = pl.BlockSpec((tm, d), lambda i: (i, 0))
    full = lambda a: pl.BlockSpec(a.shape, lambda i: (0,) * a.ndim)
    return pl.pallas_call(
        functools.partial(_ffn_kernel, ff_chunk=ff_chunk),
        grid=(n // tm,),
        in_specs=[row, full(pre_g), full(w_gate), full(w_up), full(w_down), full(post_g)],
        out_specs=row,
        out_shape=jax.ShapeDtypeStruct((n, d), jnp.float32),
        compiler_params=pltpu.CompilerParams(
            dimension_semantics=("arbitrary",), vmem_limit_bytes=VMEM_LIMIT),
        name="ffn",
    )(x, pre_g, w_gate, w_up, w_down, post_g)


def _rope(x, cos_t, sin_lo, sin_hi):
    from_hi = pltpu.roll(x, LANES - HALF_ROT, axis=1)
    from_lo = pltpu.roll(x, HALF_ROT, axis=1)
    return x * cos_t + from_hi * sin_lo + from_lo * sin_hi


def _proj_kernel(h_ref, pre_g_ref, w_in_ref, qn_g_ref, w_uq_ref, kvn_g_ref, w_ukv_ref,
                 cos_ref, sin_lo_ref, sin_hi_ref,
                 qna_ref, kna_ref, vna_ref, qm_ref, kr_ref, kv_ref, *, na_scale, mla_scale):
    hn = _rms(h_ref[...], pre_g_ref[...]).astype(jnp.bfloat16)
    z = _bdot(hn, w_in_ref[...])
    w = NA_WIDTH
    qna_ref[...] = (z[:, 0:w] * na_scale).astype(jnp.bfloat16)
    kna_ref[...] = z[:, w:2 * w].astype(jnp.bfloat16)
    vna_ref[...] = z[:, 2 * w:3 * w].astype(jnp.bfloat16)
    o = 3 * w
    c_q = z[:, o:o + MLA_Q_LORA]
    c_kv = z[:, o + MLA_Q_LORA:o + MLA_Q_LORA + MLA_KV_LORA]
    k_rope = z[:, o + MLA_Q_LORA + MLA_KV_LORA:]

    cos_t, sin_lo, sin_hi = cos_ref[...], sin_lo_ref[...], sin_hi_ref[...]
    q = _bdot(_rms(c_q, qn_g_ref[...]).astype(jnp.bfloat16), w_uq_ref[...])
    kv = _bdot(_rms(c_kv, kvn_g_ref[...]).astype(jnp.bfloat16), w_ukv_ref[...])
    k_rot = _rope(k_rope, cos_t, sin_lo, sin_hi)
    kv_ref[...] = kv.astype(jnp.bfloat16)
    lane = lax.broadcasted_iota(jnp.int32, k_rot.shape, 1)
    for h in range(MLA_HEADS):
        sl = slice(h * HEAD_SLOT, (h + 1) * HEAD_SLOT)
        qm_ref[:, sl] = (_rope(q[:, sl], cos_t, sin_lo, sin_hi) * mla_scale).astype(jnp.bfloat16)
        kr_ref[:, sl] = jnp.where(lane < MLA_QK_NOPE, kv[:, sl], k_rot).astype(jnp.bfloat16)


def _proj(h, pre_g, w_in, qn_g, w_uq, kvn_g, w_ukv, cos_t, sin_lo, sin_hi, *, seq, tm=512):
    n, d = h.shape
    assert n % tm == 0 and seq % tm == 0
    tiles_per_seq = seq // tm
    row = lambda width: pl.BlockSpec((tm, width), lambda i: (i, 0))
    full = lambda a: pl.BlockSpec(a.shape, lambda i: (0,) * a.ndim)
    tab = pl.BlockSpec((tm, LANES), lambda i: (i % tiles_per_seq, 0))
    slots = MLA_HEADS * HEAD_SLOT
    bf = jnp.bfloat16
    return pl.pallas_call(
        functools.partial(_proj_kernel, na_scale=NA_HEAD_DIM ** -0.5,
                          mla_scale=(MLA_QK_NOPE + MLA_QK_ROPE) ** -0.5),
        grid=(n // tm,),
        in_specs=[row(d), full(pre_g), full(w_in), full(qn_g), full(w_uq), full(kvn_g),
                  full(w_ukv), tab, tab, tab],
        out_specs=[row(NA_WIDTH), row(NA_WIDTH), row(NA_WIDTH), row(slots), row(slots), row(slots)],
        out_shape=[jax.ShapeDtypeStruct((n, NA_WIDTH), bf)] * 3
        + [jax.ShapeDtypeStruct((n, slots), bf)] * 3,
        compiler_params=pltpu.CompilerParams(
            dimension_semantics=("arbitrary",), vmem_limit_bytes=VMEM_LIMIT),
        name="proj",
    )(h, pre_g, w_in, qn_g, w_uq, kvn_g, w_ukv, cos_t, sin_lo, sin_hi)


def _softmax_pv(s, v):
    m = jnp.max(s, axis=-1, keepdims=True)
    p = jnp.exp(s - m)
    l = jnp.sum(p, axis=-1, keepdims=True)
    return _bdot(p.astype(jnp.bfloat16), v) * (1.0 / l)


def _na_kernel(q_ref, k_ref, v_ref, bias_ref, o_ref, *, n_rows):
    g = pl.program_id(1)
    key_row0 = jnp.clip(g * NA_GROUP_ROWS - NA_WIN_H // 2, 0, n_rows - NA_KEY_ROWS)
    start = pl.multiple_of(key_row0 * GRID_W, GRID_W)
    n_keys = NA_KEY_ROWS * GRID_W
    lane = lax.broadcasted_iota(jnp.int32, (NA_GROUP_ROWS * GRID_W, LANES), 1)
    low = lane < NA_HEAD_DIM
    for pair in range(NA_HEADS // 2):
        sl = slice(pair * LANES, (pair + 1) * LANES)
        q = q_ref[0, :, sl]
        k = k_ref[0, pl.ds(start, n_keys), sl]
        v = v_ref[0, pl.ds(start, n_keys), sl]
        outs = []
        for e in range(2):
            qe = jnp.where(low if e == 0 else ~low, q, jnp.zeros_like(q))
            s = lax.dot_general(qe, k, _NT, preferred_element_type=jnp.float32)
            outs.append(_softmax_pv(s + bias_ref[0, 2 * pair + e], v))
        o_ref[0, :, sl] = jnp.where(low, outs[0], outs[1]).astype(o_ref.dtype)


def _na_bias(rpb, n_rows):
    G, KR, W = NA_GROUP_ROWS, NA_KEY_ROWS, GRID_W
    n_groups = n_rows // G
    kh = min(NA_WIN_H, n_rows)
    qc = np.arange(W)[:, None]
    kc = np.arange(W)[None, :]
    win0 = np.clip(qc - NA_WIN_W // 2, 0, W - NA_WIN_W)
    col_ok = (kc >= win0) & (kc < win0 + NA_WIN_W)
    col_off = np.clip(kc - qc + NA_WIN_W - 1, 0, 2 * NA_WIN_W - 2)
    col_sel = jnp.asarray(np.eye(2 * NA_WIN_W - 1, dtype=np.float32)[:, col_off])
    variants = []
    for g in (0, 1, n_groups - 1):
        r = g * G + np.arange(G)[:, None]
        key_row0 = np.clip(g * G - NA_WIN_H // 2, 0, n_rows - KR)
        kr = key_row0 + np.arange(KR)[None, :]
        row0 = np.clip(r - kh // 2, 0, n_rows - kh)
        row_ok = (kr >= row0) & (kr < row0 + kh)
        row_off = np.clip(kr - r + NA_WIN_H - 1, 0, 2 * NA_WIN_H - 2)
        ok = row_ok[:, None, :, None] & col_ok[None, :, None, :]
        row_sel = jnp.asarray(np.eye(2 * NA_WIN_H - 1, dtype=np.float32)[row_off])
        b = jnp.einsum('gki,hij,jqc->hgqkc', row_sel, rpb, col_sel,
                       precision=lax.Precision.HIGHEST)
        b = jnp.where(ok[None], b, NEG_INF)
        variants.append(b.reshape(rpb.shape[0], G * W, KR * W))
    return jnp.stack(variants)


def _na_attention(q, k, v, bias):
    b, s, w = q.shape
    n_rows = s // GRID_W
    assert n_rows % NA_GROUP_ROWS == 0 and n_rows >= NA_KEY_ROWS and n_rows // NA_GROUP_ROWS >= 3
    n_groups = n_rows // NA_GROUP_ROWS
    tq = NA_GROUP_ROWS * GRID_W

    def bias_map(bi, g):
        variant = jnp.where(g == 0, 0, jnp.where(g == n_groups - 1, 2, 1))
        return (variant, 0, 0, 0)

    whole = pl.BlockSpec((1, s, w), lambda bi, g: (bi, 0, 0))
    qblk = pl.BlockSpec((1, tq, w), lambda bi, g: (bi, g, 0))
    return pl.pallas_call(
        functools.partial(_na_kernel, n_rows=n_rows),
        grid=(b, n_groups),
        in_specs=[qblk, whole, whole, pl.BlockSpec((1,) + bias.shape[1:], bias_map)],
        out_specs=qblk,
        out_shape=jax.ShapeDtypeStruct((b, s, w), jnp.bfloat16),
        compiler_params=pltpu.CompilerParams(
            dimension_semantics=("arbitrary", "arbitrary"), vmem_limit_bytes=VMEM_LIMIT),
        name="na_attn",
    )(q, k, v, bias)


def _mla_kernel(q_ref, kr_ref, kv_ref, o_ref):
    outs = []
    for e in range(2):
        sl = slice(e * HEAD_SLOT, (e + 1) * HEAD_SLOT)
        s = lax.dot_general(q_ref[0, :, sl], kr_ref[0, :, sl], _NT,
                            preferred_element_type=jnp.float32)
        outs.append(_softmax_pv(s, kv_ref[0, :, sl]))
    lane = lax.broadcasted_iota(jnp.int32, outs[0].shape, 1)
    even = pltpu.roll(outs[0], LANES - MLA_V_DIM, axis=1)
    o_ref[0] = jnp.where(lane < MLA_V_DIM, even, outs[1]).astype(o_ref.dtype)


def _mla_attention(qm, kr, kv, *, tq=512):
    b, s, slots = qm.shape
    n_pairs = MLA_HEADS // 2
    assert s % tq == 0 and slots == MLA_HEADS * HEAD_SLOT
    pair_w = 2 * HEAD_SLOT
    qblk = pl.BlockSpec((1, tq, pair_w), lambda bi, p, t: (bi, t, p))
    keys = pl.BlockSpec((1, s, pair_w), lambda bi, p, t: (bi, 0, p))
    return pl.pallas_call(
        _mla_kernel,
        grid=(b, n_pairs, s // tq),
        in_specs=[qblk, keys, keys],
        out_specs=pl.BlockSpec((1, tq, 2 * MLA_V_DIM), lambda bi, p, t: (bi, t, p)),
        out_shape=jax.ShapeDtypeStruct((b, s, MLA_WIDTH), jnp.bfloat16),
        compiler_params=pltpu.CompilerParams(
            dimension_semantics=("arbitrary", "arbitrary", "arbitrary"),
            vmem_limit_bytes=VMEM_LIMIT),
        name="mla_attn",
    )(qm, kr, kv)


def _mix_out_kernel(h_ref, ona_ref, omla_ref, na_g_ref, mla_g_ref, w_na_ref, w_mla_ref,
                    post_g_ref, o_ref):
    na = _rms(ona_ref[...].astype(jnp.float32), na_g_ref[...]).astype(jnp.bfloat16)
    mla = _rms(omla_ref[...].astype(jnp.float32), mla_g_ref[...]).astype(jnp.bfloat16)
    mixed = _bdot(na, w_na_ref[...]) + _bdot(mla, w_mla_ref[...])
    o_ref[...] = h_ref[...] + _rms(mixed, post_g_ref[...])


def _mix_out(h, o_na, o_mla, na_g, mla_g, w_na, w_mla, post_g, *, tm=512):
    n, d = h.shape
    assert n % tm == 0
    row = lambda width: pl.BlockSpec((tm, width), lambda i: (i, 0))
    full = lambda a: pl.BlockSpec(a.shape, lambda i: (0,) * a.ndim)
    return pl.pallas_call(
        _mix_out_kernel,
        grid=(n // tm,),
        in_specs=[row(d), row(o_na.shape[1]), row(o_mla.shape[1]), full(na_g), full(mla_g),
                  full(w_na), full(w_mla), full(post_g)],
        out_specs=row(d),
        out_shape=jax.ShapeDtypeStruct((n, d), jnp.float32),
        compiler_params=pltpu.CompilerParams(
            dimension_semantics=("arbitrary",), vmem_limit_bytes=VMEM_LIMIT),
        name="mix_out",
    )(h, o_na, o_mla, na_g, mla_g, w_na, w_mla, post_g)


def _rope_tables(seq):
    t = jnp.arange(seq)
    row = (t // GRID_W).astype(jnp.float32)
    col = (t % GRID_W).astype(jnp.float32)
    n_freq = MLA_QK_ROPE // 4
    inv_freq = 1.0 / (ROPE_THETA ** (jnp.arange(n_freq, dtype=jnp.float32) / n_freq))
    ang = jnp.concatenate([row[:, None] * inv_freq[None, :], col[:, None] * inv_freq[None, :]], axis=-1)
    cos, sin = jnp.cos(ang), jnp.sin(ang)
    zeros = lambda width: jnp.zeros((seq, width), jnp.float32)
    tail = LANES - ROT_LANE0 - MLA_QK_ROPE
    cos_t = jnp.concatenate([jnp.ones((seq, ROT_LANE0), jnp.float32), cos, cos, zeros(tail)], axis=-1)
    sin_lo = jnp.concatenate([zeros(ROT_LANE0), -sin, zeros(HALF_ROT + tail)], axis=-1)
    sin_hi = jnp.concatenate([zeros(ROT_LANE0 + HALF_ROT), sin, zeros(tail)], axis=-1)
    return cos_t, sin_lo, sin_hi


def _pad_cols(w, left, right):
    return jnp.pad(w, ((0, 0), (left, right)))


def kernel(x, ffn1_pre_g, ffn1_w_gu, ffn1_w_down, ffn1_post_g, mix_pre_g, w_in, na_rpb, mla_q_norm_g, mla_w_uq, mla_kv_norm_g, mla_w_ukv, na_out_norm_g, mla_out_norm_g, w_out, mix_post_g, ffn2_pre_g, ffn2_w_gu, ffn2_w_down, ffn2_post_g):
    B, S, D = x.shape
    depth = ffn1_w_gu.shape[0]
    d_ff = ffn1_w_down.shape[1]
    bf = jnp.bfloat16
    n_rows = S // GRID_W
    cos_t, sin_lo, sin_hi = _rope_tables(S)
    row2 = lambda g: g.reshape(1, -1)

    h = x.reshape(B * S, D)
    for l in range(depth):
        h = _ffn(h, row2(ffn1_pre_g[l]), ffn1_w_gu[l, :, :d_ff].astype(bf),
                 ffn1_w_gu[l, :, d_ff:].astype(bf), ffn1_w_down[l].astype(bf), row2(ffn1_post_g[l]))

        n_main = 3 * NA_WIDTH + MLA_Q_LORA + MLA_KV_LORA
        w_in_l = jnp.concatenate(
            [w_in[l, :, :n_main],
             _pad_cols(w_in[l, :, n_main:], ROT_LANE0, LANES - ROT_LANE0 - MLA_QK_ROPE)], axis=-1).astype(bf)
        qk = MLA_QK_NOPE + MLA_QK_ROPE
        w_uq_l = jnp.pad(mla_w_uq[l].reshape(MLA_Q_LORA, MLA_HEADS, qk),
                         ((0, 0), (0, 0), (0, HEAD_SLOT - qk))).reshape(MLA_Q_LORA, MLA_HEADS * HEAD_SLOT).astype(bf)
        qna, kna, vna, qm, kr, kv = _proj(
            h, row2(mix_pre_g[l]), w_in_l, row2(mla_q_norm_g[l]), w_uq_l, row2(mla_kv_norm_g[l]),
            mla_w_ukv[l].astype(bf), cos_t, sin_lo, sin_hi, seq=S)

        bias = _na_bias(na_rpb[l].astype(jnp.float32), n_rows)
        r3 = lambda a: a.reshape(B, S, a.shape[-1])
        o_na = _na_attention(r3(qna), r3(kna), r3(vna), bias)
        o_mla = _mla_attention(r3(qm), r3(kr), r3(kv))

        h = _mix_out(h, o_na.reshape(B * S, NA_WIDTH), o_mla.reshape(B * S, MLA_WIDTH),
                     row2(na_out_norm_g[l]), row2(mla_out_norm_g[l]),
                     w_out[l, :NA_WIDTH].astype(bf), w_out[l, NA_WIDTH:].astype(bf), row2(mix_post_g[l]))

        h = _ffn(h, row2(ffn2_pre_g[l]), ffn2_w_gu[l, :, :d_ff].astype(bf),
                 ffn2_w_gu[l, :, d_ff:].astype(bf), ffn2_w_down[l].astype(bf), row2(ffn2_post_g[l]))
    return h.reshape(B, S, D)
```

```python
import functools

import numpy as np
import jax
import jax.numpy as jnp
from jax import lax
from jax.experimental import pallas as pl
from jax.experimental.pallas import tpu as pltpu

GRID_W = 64
EPS = 1e-6
NEG_INF = -1e30

NA_HEADS = 8
NA_HEAD_DIM = 64
NA_WIN_H = 8
NA_WIN_W = 16
NA_WIDTH = NA_HEADS * NA_HEAD_DIM

MLA_HEADS = 8
MLA_QK_NOPE = 64
MLA_QK_ROPE = 32
MLA_V_DIM = 64
MLA_Q_LORA = 256
MLA_KV_LORA = 128
MLA_WIDTH = MLA_HEADS * MLA_V_DIM
ROPE_THETA = 10000.0

LANES = 128
HEAD_SLOT = LANES
ROT_LANE0 = MLA_QK_NOPE
HALF_ROT = MLA_QK_ROPE // 2

NA_GROUP_ROWS = 4
NA_KEY_ROWS = NA_GROUP_ROWS + NA_WIN_H - 1

VMEM_LIMIT = 56 * 1024 * 1024

_NT = (((1,), (1,)), ((), ()))
LOG2_E = float(np.log2(np.e))


def _rms(x, g):
    return x * lax.rsqrt(jnp.mean(x * x, axis=-1, keepdims=True) + EPS) * g


def _bdot(a, b):
    return jnp.dot(a, b, preferred_element_type=jnp.float32)


def _ffn_kernel(x_ref, pre_g_ref, wg_ref, wu_ref, wd_ref, post_g_ref, o_ref, *, ff_chunk):
    x = x_ref[...]
    xn = _rms(x, pre_g_ref[...]).astype(jnp.bfloat16)
    d_ff = wg_ref.shape[1]
    acc = jnp.zeros(x.shape, jnp.float32)
    for c0 in range(0, d_ff, ff_chunk):
        gate = _bdot(xn, wg_ref[:, c0:c0 + ff_chunk])
        up = _bdot(xn, wu_ref[:, c0:c0 + ff_chunk])
        act = (gate * jax.nn.sigmoid(gate) * up).astype(jnp.bfloat16)
        acc = acc + _bdot(act, wd_ref[c0:c0 + ff_chunk, :])
    o_ref[...] = x + 0.5 * _rms(acc, post_g_ref[...])


def _ffn(x, pre_g, w_gate, w_up, w_down, post_g, *, tm=512, ff_chunk=1408):
    n, d = x.shape
    d_ff = w_gate.shape[1]
    assert n % tm == 0 and d_ff % ff_chunk == 0
    row = pl.BlockSpec((tm, d), lambda i: (i, 0))
    full = lambda a: pl.BlockSpec(a.shape, lambda i: (0,) * a.ndim)
    return pl.pallas_call(
        functools.partial(_ffn_kernel, ff_chunk=ff_chunk),
        grid=(n // tm,),
        in_specs=[row, full(pre_g), full(w_gate), full(w_up), full(w_down), full(post_g)],
        out_specs=row,
        out_shape=jax.ShapeDtypeStruct((n, d), jnp.float32),
        compiler_params=pltpu.CompilerParams(
            dimension_semantics=("arbitrary",), vmem_limit_bytes=VMEM_LIMIT),
        name="ffn",
    )(x, pre_g, w_gate, w_up, w_down, post_g)


def _rope(x, cos_t, sin_lo, sin_hi):
    from_hi = pltpu.roll(x, LANES - HALF_ROT, axis=1)
    from_lo = pltpu.roll(x, HALF_ROT, axis=1)
    return x * cos_t + from_hi * sin_lo + from_lo * sin_hi


def _proj_kernel(h_ref, pre_g_ref, w_in_ref, qn_g_ref, w_uq_ref, kvn_g_ref, w_ukv_ref,
                 cos_ref, sin_lo_ref, sin_hi_ref,
                 qna_ref, kna_ref, vna_ref, qm_ref, kr_ref, v1_ref, *, na_scale, mla_scale):
    hn = _rms(h_ref[...], pre_g_ref[...]).astype(jnp.bfloat16)
    z = _bdot(hn, w_in_ref[...])
    w = NA_WIDTH
    qna_ref[...] = (z[:, 0:w] * na_scale).astype(jnp.bfloat16)
    kna_ref[...] = z[:, w:2 * w].astype(jnp.bfloat16)
    vna_ref[...] = z[:, 2 * w:3 * w].astype(jnp.bfloat16)
    o = 3 * w
    c_q = z[:, o:o + MLA_Q_LORA]
    c_kv = z[:, o + MLA_Q_LORA:o + MLA_Q_LORA + MLA_KV_LORA]
    k_rope = z[:, o + MLA_Q_LORA + MLA_KV_LORA:]

    cos_t, sin_lo, sin_hi = cos_ref[...], sin_lo_ref[...], sin_hi_ref[...]
    q = _bdot(_rms(c_q, qn_g_ref[...]).astype(jnp.bfloat16), w_uq_ref[...])
    kv = _bdot(_rms(c_kv, kvn_g_ref[...]).astype(jnp.bfloat16), w_ukv_ref[...])
    k_rot = _rope(k_rope, cos_t, sin_lo, sin_hi)
    lane = lax.broadcasted_iota(jnp.int32, k_rot.shape, 1)
    nope = lane < MLA_QK_NOPE
    for h in range(MLA_HEADS):
        sl = slice(h * HEAD_SLOT, (h + 1) * HEAD_SLOT)
        qm_ref[:, sl] = (_rope(q[:, sl], cos_t, sin_lo, sin_hi) * mla_scale).astype(jnp.bfloat16)
        kr_ref[:, sl] = jnp.where(nope, kv[:, sl], k_rot).astype(jnp.bfloat16)
        v1_ref[:, sl] = jnp.where(nope, 1.0, kv[:, sl]).astype(jnp.bfloat16)


def _proj(h, pre_g, w_in, qn_g, w_uq, kvn_g, w_ukv, cos_t, sin_lo, sin_hi, *, seq, tm=512):
    n, d = h.shape
    assert n % tm == 0 and seq % tm == 0
    tiles_per_seq = seq // tm
    row = lambda width: pl.BlockSpec((tm, width), lambda i: (i, 0))
    full = lambda a: pl.BlockSpec(a.shape, lambda i: (0,) * a.ndim)
    tab = pl.BlockSpec((tm, LANES), lambda i: (i % tiles_per_seq, 0))
    slots = MLA_HEADS * HEAD_SLOT
    bf = jnp.bfloat16
    return pl.pallas_call(
        functools.partial(_proj_kernel, na_scale=NA_HEAD_DIM ** -0.5,
                          mla_scale=(MLA_QK_NOPE + MLA_QK_ROPE) ** -0.5 * LOG2_E),
        grid=(n // tm,),
        in_specs=[row(d), full(pre_g), full(w_in), full(qn_g), full(w_uq), full(kvn_g),
                  full(w_ukv), tab, tab, tab],
        out_specs=[row(NA_WIDTH), row(NA_WIDTH), row(NA_WIDTH), row(slots), row(slots), row(slots)],
        out_shape=[jax.ShapeDtypeStruct((n, NA_WIDTH), bf)] * 3
        + [jax.ShapeDtypeStruct((n, slots), bf)] * 3,
        compiler_params=pltpu.CompilerParams(
            dimension_semantics=("arbitrary",), vmem_limit_bytes=VMEM_LIMIT),
        name="proj",
    )(h, pre_g, w_in, qn_g, w_uq, kvn_g, w_ukv, cos_t, sin_lo, sin_hi)


def _softmax_pv(s, v):
    m = jnp.max(s, axis=-1, keepdims=True)
    p = jnp.exp(s - m)
    l = jnp.sum(p, axis=-1, keepdims=True)
    return _bdot(p.astype(jnp.bfloat16), v) * (1.0 / l)


def _na_kernel(q_ref, k_ref, v_ref, bias_ref, o_ref, *, n_rows):
    g = pl.program_id(1)
    key_row0 = jnp.clip(g * NA_GROUP_ROWS - NA_WIN_H // 2, 0, n_rows - NA_KEY_ROWS)
    start = pl.multiple_of(key_row0 * GRID_W, GRID_W)
    n_keys = NA_KEY_ROWS * GRID_W
    lane = lax.broadcasted_iota(jnp.int32, (NA_GROUP_ROWS * GRID_W, LANES), 1)
    low = lane < NA_HEAD_DIM
    for pair in range(NA_HEADS // 2):
        sl = slice(pair * LANES, (pair + 1) * LANES)
        q = q_ref[0, :, sl]
        k = k_ref[0, pl.ds(start, n_keys), sl]
        v = v_ref[0, pl.ds(start, n_keys), sl]
        outs = []
        for e in range(2):
            qe = jnp.where(low if e == 0 else ~low, q, jnp.zeros_like(q))
            s = lax.dot_general(qe, k, _NT, preferred_element_type=jnp.float32)
            outs.append(_softmax_pv(s + bias_ref[0, 2 * pair + e], v))
        o_ref[0, :, sl] = jnp.where(low, outs[0], outs[1]).astype(o_ref.dtype)


def _na_bias(rpb, n_rows):
    G, KR, W = NA_GROUP_ROWS, NA_KEY_ROWS, GRID_W
    n_groups = n_rows // G
    kh = min(NA_WIN_H, n_rows)
    qc = np.arange(W)[:, None]
    kc = np.arange(W)[None, :]
    win0 = np.clip(qc - NA_WIN_W // 2, 0, W - NA_WIN_W)
    col_ok = (kc >= win0) & (kc < win0 + NA_WIN_W)
    col_off = np.clip(kc - qc + NA_WIN_W - 1, 0, 2 * NA_WIN_W - 2)
    col_sel = jnp.asarray(np.eye(2 * NA_WIN_W - 1, dtype=np.float32)[:, col_off])
    variants = []
    for g in (0, 1, n_groups - 1):
        r = g * G + np.arange(G)[:, None]
        key_row0 = np.clip(g * G - NA_WIN_H // 2, 0, n_rows - KR)
        kr = key_row0 + np.arange(KR)[None, :]
        row0 = np.clip(r - kh // 2, 0, n_rows - kh)
        row_ok = (kr >= row0) & (kr < row0 + kh)
        row_off = np.clip(kr - r + NA_WIN_H - 1, 0, 2 * NA_WIN_H - 2)
        ok = row_ok[:, None, :, None] & col_ok[None, :, None, :]
        row_sel = jnp.asarray(np.eye(2 * NA_WIN_H - 1, dtype=np.float32)[row_off])
        b = jnp.einsum('gki,hij,jqc->hgqkc', row_sel, rpb, col_sel,
                       precision=lax.Precision.HIGHEST)
        b = jnp.where(ok[None], b, NEG_INF)
        variants.append(b.reshape(rpb.shape[0], G * W, KR * W))
    return jnp.stack(variants)


def _na_attention(q, k, v, bias):
    b, s, w = q.shape
    n_rows = s // GRID_W
    assert n_rows % NA_GROUP_ROWS == 0 and n_rows >= NA_KEY_ROWS and n_rows // NA_GROUP_ROWS >= 3
    n_groups = n_rows // NA_GROUP_ROWS
    tq = NA_GROUP_ROWS * GRID_W

    def bias_map(bi, g):
        variant = jnp.where(g == 0, 0, jnp.where(g == n_groups - 1, 2, 1))
        return (variant, 0, 0, 0)

    whole = pl.BlockSpec((1, s, w), lambda bi, g: (bi, 0, 0))
    qblk = pl.BlockSpec((1, tq, w), lambda bi, g: (bi, g, 0))
    return pl.pallas_call(
        functools.partial(_na_kernel, n_rows=n_rows),
        grid=(b, n_groups),
        in_specs=[qblk, whole, whole, pl.BlockSpec((1,) + bias.shape[1:], bias_map)],
        out_specs=qblk,
        out_shape=jax.ShapeDtypeStruct((b, s, w), jnp.bfloat16),
        compiler_params=pltpu.CompilerParams(
            dimension_semantics=("arbitrary", "arbitrary"), vmem_limit_bytes=VMEM_LIMIT),
        name="na_attn",
    )(q, k, v, bias)


def _mla_kernel(q_ref, kr_ref, kv_ref, o_ref, s_ref, acc_ref, *, tk):
    n_chunks = kr_ref.shape[1] // tk
    tq = q_ref.shape[1]
    slots = [slice(e * HEAD_SLOT, (e + 1) * HEAD_SLOT) for e in range(2)]

    def score_chunk(c, m):
        keys = pl.ds(pl.multiple_of(c * tk, tk), tk)
        new_m = []
        for e, sl in enumerate(slots):
            s = lax.dot_general(q_ref[0, :, sl], kr_ref[0, keys, sl], _NT,
                                preferred_element_type=jnp.float32)
            s_ref[e, c] = s
            new_m.append(jnp.maximum(m[e], jnp.max(s, axis=-1, keepdims=True)))
        return tuple(new_m)

    m = lax.fori_loop(0, n_chunks, score_chunk,
                      tuple(jnp.full((tq, 1), -jnp.inf, jnp.float32) for _ in slots))

    acc_ref[...] = jnp.zeros(acc_ref.shape, jnp.float32)

    @pl.loop(0, n_chunks)
    def _(c):
        keys = pl.ds(pl.multiple_of(c * tk, tk), tk)
        for e, sl in enumerate(slots):
            p = jnp.exp2(s_ref[e, c] - m[e])
            acc_ref[e] += _bdot(p.astype(jnp.bfloat16), kv_ref[0, keys, sl])

    lane = lax.broadcasted_iota(jnp.int32, (tq, HEAD_SLOT), 1)
    acc = [acc_ref[e] for e in range(2)]
    swapped = [pltpu.roll(a, LANES - MLA_V_DIM, axis=1) for a in acc]
    o_ref[0] = jnp.where(lane < MLA_V_DIM, swapped[0] / acc[0], acc[1] / swapped[1]).astype(o_ref.dtype)


def _mla_attention(qm, kr, kv, *, tq=512, tk=4096):
    b, s, slots = qm.shape
    n_pairs = MLA_HEADS // 2
    assert s % tq == 0 and s % tk == 0 and slots == MLA_HEADS * HEAD_SLOT
    pair_w = 2 * HEAD_SLOT
    qblk = pl.BlockSpec((1, tq, pair_w), lambda bi, p, t: (bi, t, p))
    keys = pl.BlockSpec((1, s, pair_w), lambda bi, p, t: (bi, 0, p))
    return pl.pallas_call(
        functools.partial(_mla_kernel, tk=tk),
        grid=(b, n_pairs, s // tq),
        in_specs=[qblk, keys, keys],
        out_specs=pl.BlockSpec((1, tq, 2 * MLA_V_DIM), lambda bi, p, t: (bi, t, p)),
        out_shape=jax.ShapeDtypeStruct((b, s, MLA_WIDTH), jnp.bfloat16),
        scratch_shapes=[pltpu.VMEM((2, s // tk, tq, tk), jnp.float32),
                        pltpu.VMEM((2, tq, HEAD_SLOT), jnp.float32)],
        compiler_params=pltpu.CompilerParams(
            dimension_semantics=("arbitrary", "arbitrary", "arbitrary"),
            vmem_limit_bytes=VMEM_LIMIT),
        name="mla_attn",
    )(qm, kr, kv)


def _mix_out_kernel(h_ref, ona_ref, omla_ref, na_g_ref, mla_g_ref, w_na_ref, w_mla_ref,
                    post_g_ref, o_ref):
    na = _rms(ona_ref[...].astype(jnp.float32), na_g_ref[...]).astype(jnp.bfloat16)
    mla = _rms(omla_ref[...].astype(jnp.float32), mla_g_ref[...]).astype(jnp.bfloat16)
    mixed = _bdot(na, w_na_ref[...]) + _bdot(mla, w_mla_ref[...])
    o_ref[...] = h_ref[...] + _rms(mixed, post_g_ref[...])


def _mix_out(h, o_na, o_mla, na_g, mla_g, w_na, w_mla, post_g, *, tm=512):
    n, d = h.shape
    assert n % tm == 0
    row = lambda width: pl.BlockSpec((tm, width), lambda i: (i, 0))
    full = lambda a: pl.BlockSpec(a.shape, lambda i: (0,) * a.ndim)
    return pl.pallas_call(
        _mix_out_kernel,
        grid=(n // tm,),
        in_specs=[row(d), row(o_na.shape[1]), row(o_mla.shape[1]), full(na_g), full(mla_g),
                  full(w_na), full(w_mla), full(post_g)],
        out_specs=row(d),
        out_shape=jax.ShapeDtypeStruct((n, d), jnp.float32),
        compiler_params=pltpu.CompilerParams(
            dimension_semantics=("arbitrary",), vmem_limit_bytes=VMEM_LIMIT),
        name="mix_out",
    )(h, o_na, o_mla, na_g, mla_g, w_na, w_mla, post_g)


def _rope_tables(seq):
    t = jnp.arange(seq)
    row = (t // GRID_W).astype(jnp.float32)
    col = (t % GRID_W).astype(jnp.float32)
    n_freq = MLA_QK_ROPE // 4
    inv_freq = 1.0 / (ROPE_THETA ** (jnp.arange(n_freq, dtype=jnp.float32) / n_freq))
    ang = jnp.concatenate([row[:, None] * inv_freq[None, :], col[:, None] * inv_freq[None, :]], axis=-1)
    cos, sin = jnp.cos(ang), jnp.sin(ang)
    zeros = lambda width: jnp.zeros((seq, width), jnp.float32)
    tail = LANES - ROT_LANE0 - MLA_QK_ROPE
    cos_t = jnp.concatenate([jnp.ones((seq, ROT_LANE0), jnp.float32), cos, cos, zeros(tail)], axis=-1)
    sin_lo = jnp.concatenate([zeros(ROT_LANE0), -sin, zeros(HALF_ROT + tail)], axis=-1)
    sin_hi = jnp.concatenate([zeros(ROT_LANE0 + HALF_ROT), sin, zeros(tail)], axis=-1)
    return cos_t, sin_lo, sin_hi


def _pad_cols(w, left, right):
    return jnp.pad(w, ((0, 0), (left, right)))


def kernel(x, ffn1_pre_g, ffn1_w_gu, ffn1_w_down, ffn1_post_g, mix_pre_g, w_in, na_rpb, mla_q_norm_g, mla_w_uq, mla_kv_norm_g, mla_w_ukv, na_out_norm_g, mla_out_norm_g, w_out, mix_post_g, ffn2_pre_g, ffn2_w_gu, ffn2_w_down, ffn2_post_g):
    B, S, D = x.shape
    depth = ffn1_w_gu.shape[0]
    d_ff = ffn1_w_down.shape[1]
    bf = jnp.bfloat16
    n_rows = S // GRID_W
    cos_t, sin_lo, sin_hi = _rope_tables(S)
    row2 = lambda g: g.reshape(1, -1)

    h = x.reshape(B * S, D)
    for l in range(depth):
        h = _ffn(h, row2(ffn1_pre_g[l]), ffn1_w_gu[l, :, :d_ff].astype(bf),
                 ffn1_w_gu[l, :, d_ff:].astype(bf), ffn1_w_down[l].astype(bf), row2(ffn1_post_g[l]))

        n_main = 3 * NA_WIDTH + MLA_Q_LORA + MLA_KV_LORA
        w_in_l = jnp.concatenate(
            [w_in[l, :, :n_main],
             _pad_cols(w_in[l, :, n_main:], ROT_LANE0, LANES - ROT_LANE0 - MLA_QK_ROPE)], axis=-1).astype(bf)
        qk = MLA_QK_NOPE + MLA_QK_ROPE
        w_uq_l = jnp.pad(mla_w_uq[l].reshape(MLA_Q_LORA, MLA_HEADS, qk),
                         ((0, 0), (0, 0), (0, HEAD_SLOT - qk))).reshape(MLA_Q_LORA, MLA_HEADS * HEAD_SLOT).astype(bf)
        qna, kna, vna, qm, kr, kv = _proj(
            h, row2(mix_pre_g[l]), w_in_l, row2(mla_q_norm_g[l]), w_uq_l, row2(mla_kv_norm_g[l]),
            mla_w_ukv[l].astype(bf), cos_t, sin_lo, sin_hi, seq=S)

        bias = _na_bias(na_rpb[l].astype(jnp.float32), n_rows)
        r3 = lambda a: a.reshape(B, S, a.shape[-1])
        o_na = _na_attention(r3(qna), r3(kna), r3(vna), bias)
        o_mla = _mla_attention(r3(qm), r3(kr), r3(kv))

        h = _mix_out(h, o_na.reshape(B * S, NA_WIDTH), o_mla.reshape(B * S, MLA_WIDTH),
                     row2(na_out_norm_g[l]), row2(mla_out_norm_g[l]),
                     w_out[l, :NA_WIDTH].astype(bf), w_out[l, NA_WIDTH:].astype(bf), row2(mix_post_g[l]))

        h = _ffn(h, row2(ffn2_pre_g[l]), ffn2_w_gu[l, :, :d_ff].astype(bf),
                 ffn2_w_gu[l, :, d_ff:].astype(bf), ffn2_w_down[l].astype(bf), row2(ffn2_post_g[l]))
    return h.reshape(B, S, D)
```

```python
import functools

import numpy as np
import jax
import jax.numpy as jnp
from jax import lax
from jax.experimental import pallas as pl
from jax.experimental.pallas import tpu as pltpu

GRID_W = 64
EPS = 1e-6
NEG_INF = -1e30

NA_HEADS = 8
NA_HEAD_DIM = 64
NA_WIN_H = 8
NA_WIN_W = 16
NA_WIDTH = NA_HEADS * NA_HEAD_DIM

MLA_HEADS = 8
MLA_QK_NOPE = 64
MLA_QK_ROPE = 32
MLA_V_DIM = 64
MLA_Q_LORA = 256
MLA_KV_LORA = 128
MLA_WIDTH = MLA_HEADS * MLA_V_DIM
ROPE_THETA = 10000.0

LANES = 128
HEAD_SLOT = LANES
ROT_LANE0 = MLA_QK_NOPE
HALF_ROT = MLA_QK_ROPE // 2

NA_GROUP_ROWS = 4
NA_KEY_ROWS = NA_GROUP_ROWS + NA_WIN_H - 1

VMEM_LIMIT = 56 * 1024 * 1024

_NT = (((1,), (1,)), ((), ()))
LOG2_E = float(np.log2(np.e))


def _rms(x, g):
    return x * lax.rsqrt(jnp.mean(x * x, axis=-1, keepdims=True) + EPS) * g


def _bdot(a, b):
    return jnp.dot(a, b, preferred_element_type=jnp.float32)


def _token_rows(seq, tm, width):
    tiles = seq // tm
    return pl.BlockSpec((None, tm, width), lambda i: (i // tiles, i % tiles, 0))


def _resident(a):
    return pl.BlockSpec(a.shape, lambda i: (0,) * a.ndim)


def _ffn_kernel(x_ref, pre_g_ref, wg_ref, wu_ref, wd_ref, post_g_ref, o_ref, *, ff_chunk):
    x = x_ref[...]
    xn = _rms(x, pre_g_ref[...]).astype(jnp.bfloat16)
    d_ff = wg_ref.shape[1]
    acc = jnp.zeros(x.shape, jnp.float32)
    for c0 in range(0, d_ff, ff_chunk):
        gate = _bdot(xn, wg_ref[:, c0:c0 + ff_chunk])
        up = _bdot(xn, wu_ref[:, c0:c0 + ff_chunk])
        act = (gate * jax.nn.sigmoid(gate) * up).astype(jnp.bfloat16)
        acc = acc + _bdot(act, wd_ref[c0:c0 + ff_chunk, :])
    o_ref[...] = x + 0.5 * _rms(acc, post_g_ref[...])


def _ffn(x, pre_g, w_gate, w_up, w_down, post_g, *, tm=512, ff_chunk=1408):
    b, s, d = x.shape
    d_ff = w_gate.shape[1]
    assert s % tm == 0 and d_ff % ff_chunk == 0
    row = _token_rows(s, tm, d)
    full = _resident
    return pl.pallas_call(
        functools.partial(_ffn_kernel, ff_chunk=ff_chunk),
        grid=(b * s // tm,),
        in_specs=[row, full(pre_g), full(w_gate), full(w_up), full(w_down), full(post_g)],
        out_specs=row,
        out_shape=jax.ShapeDtypeStruct((b, s, d), jnp.float32),
        compiler_params=pltpu.CompilerParams(
            dimension_semantics=("arbitrary",), vmem_limit_bytes=VMEM_LIMIT),
        name="ffn",
    )(x, pre_g, w_gate, w_up, w_down, post_g)


def _rope(x, cos_t, sin_lo, sin_hi):
    from_hi = pltpu.roll(x, LANES - HALF_ROT, axis=1)
    from_lo = pltpu.roll(x, HALF_ROT, axis=1)
    return x * cos_t + from_hi * sin_lo + from_lo * sin_hi


def _proj_kernel(h_ref, pre_g_ref, w_in_ref, qn_g_ref, w_uq_ref, kvn_g_ref, w_ukv_ref,
                 cos_ref, sin_lo_ref, sin_hi_ref,
                 qna_ref, kna_ref, vna_ref, qm_ref, kr_ref, v1_ref, *, na_scale, mla_scale):
    hn = _rms(h_ref[...], pre_g_ref[...]).astype(jnp.bfloat16)
    z = _bdot(hn, w_in_ref[...])
    w = NA_WIDTH
    qna_ref[...] = (z[:, 0:w] * na_scale).astype(jnp.bfloat16)
    kna_ref[...] = z[:, w:2 * w].astype(jnp.bfloat16)
    vna_ref[...] = z[:, 2 * w:3 * w].astype(jnp.bfloat16)
    o = 3 * w
    c_q = z[:, o:o + MLA_Q_LORA]
    c_kv = z[:, o + MLA_Q_LORA:o + MLA_Q_LORA + MLA_KV_LORA]
    k_rope = z[:, o + MLA_Q_LORA + MLA_KV_LORA:]

    cos_t, sin_lo, sin_hi = cos_ref[...], sin_lo_ref[...], sin_hi_ref[...]
    q = _bdot(_rms(c_q, qn_g_ref[...]).astype(jnp.bfloat16), w_uq_ref[...])
    kv = _bdot(_rms(c_kv, kvn_g_ref[...]).astype(jnp.bfloat16), w_ukv_ref[...])
    k_rot = _rope(k_rope, cos_t, sin_lo, sin_hi)
    lane = lax.broadcasted_iota(jnp.int32, k_rot.shape, 1)
    nope = lane < MLA_QK_NOPE
    for h in range(MLA_HEADS):
        sl = slice(h * HEAD_SLOT, (h + 1) * HEAD_SLOT)
        qm_ref[:, sl] = (_rope(q[:, sl], cos_t, sin_lo, sin_hi) * mla_scale).astype(jnp.bfloat16)
        kr_ref[:, sl] = jnp.where(nope, kv[:, sl], k_rot).astype(jnp.bfloat16)
        v1_ref[:, sl] = jnp.where(nope, 1.0, kv[:, sl]).astype(jnp.bfloat16)


def _proj(h, pre_g, w_in, qn_g, w_uq, kvn_g, w_ukv, cos_t, sin_lo, sin_hi, *, tm=512):
    b, s, d = h.shape
    assert s % tm == 0
    tiles_per_seq = s // tm
    row = functools.partial(_token_rows, s, tm)
    full = _resident
    tab = pl.BlockSpec((tm, LANES), lambda i: (i % tiles_per_seq, 0))
    slots = MLA_HEADS * HEAD_SLOT
    bf = jnp.bfloat16
    return pl.pallas_call(
        functools.partial(_proj_kernel, na_scale=NA_HEAD_DIM ** -0.5 * LOG2_E,
                          mla_scale=(MLA_QK_NOPE + MLA_QK_ROPE) ** -0.5 * LOG2_E),
        grid=(b * s // tm,),
        in_specs=[row(d), full(pre_g), full(w_in), full(qn_g), full(w_uq), full(kvn_g),
                  full(w_ukv), tab, tab, tab],
        out_specs=[row(NA_WIDTH), row(NA_WIDTH), row(NA_WIDTH), row(slots), row(slots), row(slots)],
        out_shape=[jax.ShapeDtypeStruct((b, s, NA_WIDTH), bf)] * 3
        + [jax.ShapeDtypeStruct((b, s, slots), bf)] * 3,
        compiler_params=pltpu.CompilerParams(
            dimension_semantics=("arbitrary",), vmem_limit_bytes=VMEM_LIMIT),
        name="proj",
    )(h, pre_g, w_in, qn_g, w_uq, kvn_g, w_ukv, cos_t, sin_lo, sin_hi)


def _na_kernel(q_ref, k_ref, v_ref, bias_ref, o_ref, *, n_rows):
    g = pl.program_id(1)
    key_row0 = jnp.clip(g * NA_GROUP_ROWS - NA_WIN_H // 2, 0, n_rows - NA_KEY_ROWS)
    start = pl.multiple_of(key_row0 * GRID_W, GRID_W)
    n_keys = NA_KEY_ROWS * GRID_W
    lane = lax.broadcasted_iota(jnp.int32, (NA_GROUP_ROWS * GRID_W, LANES), 1)
    low = lane < NA_HEAD_DIM
    ones = jnp.ones((n_keys, LANES), jnp.bfloat16)
    for pair in range(NA_HEADS // 2):
        sl = slice(pair * LANES, (pair + 1) * LANES)
        q = q_ref[0, :, sl]
        k = k_ref[0, pl.ds(start, n_keys), sl]
        v1 = jnp.concatenate([v_ref[0, pl.ds(start, n_keys), sl], ones], axis=1)
        outs = []
        for e in range(2):
            qe = jnp.where(low if e == 0 else ~low, q, jnp.zeros_like(q))
            s = lax.dot_general(qe, k, _NT, preferred_element_type=jnp.float32)
            s = s + bias_ref[0, 2 * pair + e]
            p = jnp.exp2(s - jnp.max(s, axis=-1, keepdims=True))
            o = _bdot(p.astype(jnp.bfloat16), v1)
            outs.append(o[:, :LANES] / o[:, LANES:])
        o_ref[0, :, sl] = jnp.where(low, outs[0], outs[1]).astype(o_ref.dtype)


def _na_bias(rpb, n_rows):
    G, KR, W = NA_GROUP_ROWS, NA_KEY_ROWS, GRID_W
    n_groups = n_rows // G
    kh = min(NA_WIN_H, n_rows)
    qc = np.arange(W)[:, None]
    kc = np.arange(W)[None, :]
    win0 = np.clip(qc - NA_WIN_W // 2, 0, W - NA_WIN_W)
    col_ok = (kc >= win0) & (kc < win0 + NA_WIN_W)
    col_off = np.clip(kc - qc + NA_WIN_W - 1, 0, 2 * NA_WIN_W - 2)
    col_sel = jnp.asarray(np.eye(2 * NA_WIN_W - 1, dtype=np.float32)[:, col_off])
    variants = []
    for g in (0, 1, n_groups - 1):
        r = g * G + np.arange(G)[:, None]
        key_row0 = np.clip(g * G - NA_WIN_H // 2, 0, n_rows - KR)
        kr = key_row0 + np.arange(KR)[None, :]
        row0 = np.clip(r - kh // 2, 0, n_rows - kh)
        row_ok = (kr >= row0) & (kr < row0 + kh)
        row_off = np.clip(kr - r + NA_WIN_H - 1, 0, 2 * NA_WIN_H - 2)
        ok = row_ok[:, None, :, None] & col_ok[None, :, None, :]
        row_sel = jnp.asarray(np.eye(2 * NA_WIN_H - 1, dtype=np.float32)[row_off])
        b = jnp.einsum('gki,hij,jqc->hgqkc', row_sel, rpb, col_sel,
                       precision=lax.Precision.HIGHEST)
        b = jnp.where(ok[None], b * LOG2_E, NEG_INF)
        variants.append(b.reshape(rpb.shape[0], G * W, KR * W))
    return jnp.stack(variants)


def _na_attention(q, k, v, bias):
    b, s, w = q.shape
    n_rows = s // GRID_W
    assert n_rows % NA_GROUP_ROWS == 0 and n_rows >= NA_KEY_ROWS and n_rows // NA_GROUP_ROWS >= 3
    n_groups = n_rows // NA_GROUP_ROWS
    tq = NA_GROUP_ROWS * GRID_W

    def bias_map(bi, g):
        variant = jnp.where(g == 0, 0, jnp.where(g == n_groups - 1, 2, 1))
        return (variant, 0, 0, 0)

    whole = pl.BlockSpec((1, s, w), lambda bi, g: (bi, 0, 0))
    qblk = pl.BlockSpec((1, tq, w), lambda bi, g: (bi, g, 0))
    return pl.pallas_call(
        functools.partial(_na_kernel, n_rows=n_rows),
        grid=(b, n_groups),
        in_specs=[qblk, whole, whole, pl.BlockSpec((1,) + bias.shape[1:], bias_map)],
        out_specs=qblk,
        out_shape=jax.ShapeDtypeStruct((b, s, w), jnp.bfloat16),
        compiler_params=pltpu.CompilerParams(
            dimension_semantics=("arbitrary", "arbitrary"), vmem_limit_bytes=VMEM_LIMIT),
        name="na_attn",
    )(q, k, v, bias)


def _mla_kernel(q_ref, kr_ref, kv_ref, o_ref, s_ref, acc_ref, *, tk):
    n_chunks = kr_ref.shape[1] // tk
    tq = q_ref.shape[1]
    slots = [slice(e * HEAD_SLOT, (e + 1) * HEAD_SLOT) for e in range(2)]

    def score_chunk(c, m):
        keys = pl.ds(pl.multiple_of(c * tk, tk), tk)
        new_m = []
        for e, sl in enumerate(slots):
            s = lax.dot_general(q_ref[0, :, sl], kr_ref[0, keys, sl], _NT,
                                preferred_element_type=jnp.float32)
            s_ref[e, c] = s
            new_m.append(jnp.maximum(m[e], jnp.max(s, axis=-1, keepdims=True)))
        return tuple(new_m)

    m = lax.fori_loop(0, n_chunks, score_chunk,
                      tuple(jnp.full((tq, 1), -jnp.inf, jnp.float32) for _ in slots))

    acc_ref[...] = jnp.zeros(acc_ref.shape, jnp.float32)

    @pl.loop(0, n_chunks)
    def _(c):
        keys = pl.ds(pl.multiple_of(c * tk, tk), tk)
        for e, sl in enumerate(slots):
            p = jnp.exp2(s_ref[e, c] - m[e])
            acc_ref[e] += _bdot(p.astype(jnp.bfloat16), kv_ref[0, keys, sl])

    lane = lax.broadcasted_iota(jnp.int32, (tq, HEAD_SLOT), 1)
    acc = [acc_ref[e] for e in range(2)]
    swapped = [pltpu.roll(a, LANES - MLA_V_DIM, axis=1) for a in acc]
    o_ref[0] = jnp.where(lane < MLA_V_DIM, swapped[0] / acc[0], acc[1] / swapped[1]).astype(o_ref.dtype)


def _mla_attention(qm, kr, kv, *, tq=512, tk=4096):
    b, s, slots = qm.shape
    n_pairs = MLA_HEADS // 2
    assert s % tq == 0 and s % tk == 0 and slots == MLA_HEADS * HEAD_SLOT
    pair_w = 2 * HEAD_SLOT
    qblk = pl.BlockSpec((1, tq, pair_w), lambda bi, p, t: (bi, t, p))
    keys = pl.BlockSpec((1, s, pair_w), lambda bi, p, t: (bi, 0, p))
    return pl.pallas_call(
        functools.partial(_mla_kernel, tk=tk),
        grid=(b, n_pairs, s // tq),
        in_specs=[qblk, keys, keys],
        out_specs=pl.BlockSpec((1, tq, 2 * MLA_V_DIM), lambda bi, p, t: (bi, t, p)),
        out_shape=jax.ShapeDtypeStruct((b, s, MLA_WIDTH), jnp.bfloat16),
        scratch_shapes=[pltpu.VMEM((2, s // tk, tq, tk), jnp.float32),
                        pltpu.VMEM((2, tq, HEAD_SLOT), jnp.float32)],
        compiler_params=pltpu.CompilerParams(
            dimension_semantics=("arbitrary", "arbitrary", "arbitrary"),
            vmem_limit_bytes=VMEM_LIMIT),
        name="mla_attn",
    )(qm, kr, kv)


def _mix_out_kernel(h_ref, ona_ref, omla_ref, na_g_ref, mla_g_ref, w_na_ref, w_mla_ref,
                    post_g_ref, o_ref):
    na = _rms(ona_ref[...].astype(jnp.float32), na_g_ref[...]).astype(jnp.bfloat16)
    mla = _rms(omla_ref[...].astype(jnp.float32), mla_g_ref[...]).astype(jnp.bfloat16)
    mixed = _bdot(na, w_na_ref[...]) + _bdot(mla, w_mla_ref[...])
    o_ref[...] = h_ref[...] + _rms(mixed, post_g_ref[...])


def _mix_out(h, o_na, o_mla, na_g, mla_g, w_na, w_mla, post_g, *, tm=512):
    b, s, d = h.shape
    assert s % tm == 0
    row = functools.partial(_token_rows, s, tm)
    full = _resident
    return pl.pallas_call(
        _mix_out_kernel,
        grid=(b * s // tm,),
        in_specs=[row(d), row(o_na.shape[-1]), row(o_mla.shape[-1]), full(na_g), full(mla_g),
                  full(w_na), full(w_mla), full(post_g)],
        out_specs=row(d),
        out_shape=jax.ShapeDtypeStruct((b, s, d), jnp.float32),
        compiler_params=pltpu.CompilerParams(
            dimension_semantics=("arbitrary",), vmem_limit_bytes=VMEM_LIMIT),
        name="mix_out",
    )(h, o_na, o_mla, na_g, mla_g, w_na, w_mla, post_g)


def _rope_tables(seq):
    t = jnp.arange(seq)
    row = (t // GRID_W).astype(jnp.float32)
    col = (t % GRID_W).astype(jnp.float32)
    n_freq = MLA_QK_ROPE // 4
    inv_freq = 1.0 / (ROPE_THETA ** (jnp.arange(n_freq, dtype=jnp.float32) / n_freq))
    ang = jnp.concatenate([row[:, None] * inv_freq[None, :], col[:, None] * inv_freq[None, :]], axis=-1)
    cos, sin = jnp.cos(ang), jnp.sin(ang)
    zeros = lambda width: jnp.zeros((seq, width), jnp.float32)
    tail = LANES - ROT_LANE0 - MLA_QK_ROPE
    cos_t = jnp.concatenate([jnp.ones((seq, ROT_LANE0), jnp.float32), cos, cos, zeros(tail)], axis=-1)
    sin_lo = jnp.concatenate([zeros(ROT_LANE0), -sin, zeros(HALF_ROT + tail)], axis=-1)
    sin_hi = jnp.concatenate([zeros(ROT_LANE0 + HALF_ROT), sin, zeros(tail)], axis=-1)
    return cos_t, sin_lo, sin_hi


def _pad_cols(w, left, right):
    return jnp.pad(w, ((0, 0), (left, right)))


def kernel(x, ffn1_pre_g, ffn1_w_gu, ffn1_w_down, ffn1_post_g, mix_pre_g, w_in, na_rpb, mla_q_norm_g, mla_w_uq, mla_kv_norm_g, mla_w_ukv, na_out_norm_g, mla_out_norm_g, w_out, mix_post_g, ffn2_pre_g, ffn2_w_gu, ffn2_w_down, ffn2_post_g):
    B, S, D = x.shape
    depth = ffn1_w_gu.shape[0]
    d_ff = ffn1_w_down.shape[1]
    bf = jnp.bfloat16
    n_rows = S // GRID_W
    cos_t, sin_lo, sin_hi = _rope_tables(S)
    row2 = lambda g: g.reshape(1, -1)

    h = x
    for l in range(depth):
        h = _ffn(h, row2(ffn1_pre_g[l]), ffn1_w_gu[l, :, :d_ff].astype(bf),
                 ffn1_w_gu[l, :, d_ff:].astype(bf), ffn1_w_down[l].astype(bf), row2(ffn1_post_g[l]))

        n_main = 3 * NA_WIDTH + MLA_Q_LORA + MLA_KV_LORA
        w_in_l = jnp.concatenate(
            [w_in[l, :, :n_main],
             _pad_cols(w_in[l, :, n_main:], ROT_LANE0, LANES - ROT_LANE0 - MLA_QK_ROPE)], axis=-1).astype(bf)
        qk = MLA_QK_NOPE + MLA_QK_ROPE
        w_uq_l = jnp.pad(mla_w_uq[l].reshape(MLA_Q_LORA, MLA_HEADS, qk),
                         ((0, 0), (0, 0), (0, HEAD_SLOT - qk))).reshape(MLA_Q_LORA, MLA_HEADS * HEAD_SLOT).astype(bf)
        qna, kna, vna, qm, kr, v1 = _proj(
            h, row2(mix_pre_g[l]), w_in_l, row2(mla_q_norm_g[l]), w_uq_l, row2(mla_kv_norm_g[l]),
            mla_w_ukv[l].astype(bf), cos_t, sin_lo, sin_hi)

        bias = _na_bias(na_rpb[l].astype(jnp.float32), n_rows)
        o_na = _na_attention(qna, kna, vna, bias)
        o_mla = _mla_attention(qm, kr, v1)

        h = _mix_out(h, o_na, o_mla, row2(na_out_norm_g[l]), row2(mla_out_norm_g[l]),
                     w_out[l, :NA_WIDTH].astype(bf), w_out[l, NA_WIDTH:].astype(bf), row2(mix_post_g[l]))

        h = _ffn(h, row2(ffn2_pre_g[l]), ffn2_w_gu[l, :, :d_ff].astype(bf),
                 ffn2_w_gu[l, :, d_ff:].astype(bf), ffn2_w_down[l].astype(bf), row2(ffn2_post_g[l]))
    return h
```

```python
import functools

import numpy as np
import jax
import jax.numpy as jnp
from jax import lax
from jax.experimental import pallas as pl
from jax.experimental.pallas import tpu as pltpu

GRID_W = 64
EPS = 1e-6
NEG_INF = -1e30

NA_HEADS = 8
NA_HEAD_DIM = 64
NA_WIN_H = 8
NA_WIN_W = 16
NA_WIDTH = NA_HEADS * NA_HEAD_DIM

MLA_HEADS = 8
MLA_QK_NOPE = 64
MLA_QK_ROPE = 32
MLA_V_DIM = 64
MLA_Q_LORA = 256
MLA_KV_LORA = 128
MLA_WIDTH = MLA_HEADS * MLA_V_DIM
ROPE_THETA = 10000.0

LANES = 128
HEAD_SLOT = LANES
ROT_LANE0 = MLA_QK_NOPE
HALF_ROT = MLA_QK_ROPE // 2

NA_GROUP_ROWS = 4
NA_KEY_ROWS = NA_GROUP_ROWS + NA_WIN_H - 1

VMEM_LIMIT = 56 * 1024 * 1024

_NT = (((1,), (1,)), ((), ()))
LOG2_E = float(np.log2(np.e))


def _rms(x, g):
    return x * lax.rsqrt(jnp.mean(x * x, axis=-1, keepdims=True) + EPS) * g


def _bdot(a, b):
    return jnp.dot(a, b, preferred_element_type=jnp.float32)


def _token_rows(seq, tm, width):
    tiles = seq // tm
    return pl.BlockSpec((None, tm, width), lambda i: (i // tiles, i % tiles, 0))


def _resident(a):
    return pl.BlockSpec(a.shape, lambda i: (0,) * a.ndim)


def _ffn_half_step(x, pre_g_ref, wg_ref, wu_ref, wd_ref, post_g_ref, ff_chunk):
    xn = _rms(x, pre_g_ref[...]).astype(jnp.bfloat16)
    d_ff = wg_ref.shape[1]
    acc = jnp.zeros(x.shape, jnp.float32)
    for c0 in range(0, d_ff, ff_chunk):
        gate = _bdot(xn, wg_ref[:, c0:c0 + ff_chunk])
        up = _bdot(xn, wu_ref[:, c0:c0 + ff_chunk])
        act = (gate * jax.nn.sigmoid(gate) * up).astype(jnp.bfloat16)
        acc = acc + _bdot(act, wd_ref[c0:c0 + ff_chunk, :])
    return x + 0.5 * _rms(acc, post_g_ref[...])


def _ffn_kernel(x_ref, pre_g_ref, wg_ref, wu_ref, wd_ref, post_g_ref, o_ref, *, ff_chunk):
    o_ref[...] = _ffn_half_step(x_ref[...], pre_g_ref, wg_ref, wu_ref, wd_ref, post_g_ref, ff_chunk)


def _mix_ffn_kernel(h_ref, ona_ref, omla_ref, na_g_ref, mla_g_ref, w_na_ref, w_mla_ref, mix_post_g_ref,
                    pre_g_ref, wg_ref, wu_ref, wd_ref, post_g_ref, o_ref, *, ff_chunk):
    na = _rms(ona_ref[...].astype(jnp.float32), na_g_ref[...]).astype(jnp.bfloat16)
    mla = _rms(omla_ref[...].astype(jnp.float32), mla_g_ref[...]).astype(jnp.bfloat16)
    mixed = _bdot(na, w_na_ref[...]) + _bdot(mla, w_mla_ref[...])
    h = h_ref[...] + _rms(mixed, mix_post_g_ref[...])
    o_ref[...] = _ffn_half_step(h, pre_g_ref, wg_ref, wu_ref, wd_ref, post_g_ref, ff_chunk)


def _ffn(x, ffn_params, mix=None, *, tm=512, ff_chunk=1408):
    b, s, d = x.shape
    d_ff = ffn_params[1].shape[1]
    assert s % tm == 0 and d_ff % ff_chunk == 0
    row = functools.partial(_token_rows, s, tm)
    if mix is None:
        body, operands, specs = _ffn_kernel, [x], [row(d)]
    else:
        body, operands = _mix_ffn_kernel, [x, *mix]
        specs = [row(d), row(mix[0].shape[-1]), row(mix[1].shape[-1])] + [_resident(a) for a in mix[2:]]
    return pl.pallas_call(
        functools.partial(body, ff_chunk=ff_chunk),
        grid=(b * s // tm,),
        in_specs=specs + [_resident(a) for a in ffn_params],
        out_specs=row(d),
        out_shape=jax.ShapeDtypeStruct((b, s, d), jnp.float32),
        compiler_params=pltpu.CompilerParams(
            dimension_semantics=("arbitrary",), vmem_limit_bytes=VMEM_LIMIT),
        name="ffn" if mix is None else "mix_ffn",
    )(*operands, *ffn_params)


def _rope(x, cos_t, sin_lo, sin_hi):
    from_hi = pltpu.roll(x, LANES - HALF_ROT, axis=1)
    from_lo = pltpu.roll(x, HALF_ROT, axis=1)
    return x * cos_t + from_hi * sin_lo + from_lo * sin_hi


def _proj_kernel(h_ref, pre_g_ref, w_in_ref, qn_g_ref, w_uq_ref, kvn_g_ref, w_ukv_ref,
                 cos_ref, sin_lo_ref, sin_hi_ref,
                 qna_ref, kna_ref, vna_ref, qm_ref, kr_ref, v1_ref, *, na_scale, mla_scale):
    hn = _rms(h_ref[...], pre_g_ref[...]).astype(jnp.bfloat16)
    w = NA_WIDTH
    z_lat = _bdot(hn, w_in_ref[:, 3 * w:])
    c_q = z_lat[:, :MLA_Q_LORA]
    c_kv = z_lat[:, MLA_Q_LORA:MLA_Q_LORA + MLA_KV_LORA]
    k_rope = z_lat[:, MLA_Q_LORA + MLA_KV_LORA:]

    cos_t, sin_lo, sin_hi = cos_ref[...], sin_lo_ref[...], sin_hi_ref[...]
    q = _bdot(_rms(c_q, qn_g_ref[...]).astype(jnp.bfloat16), w_uq_ref[...])
    kv = _bdot(_rms(c_kv, kvn_g_ref[...]).astype(jnp.bfloat16), w_ukv_ref[...])
    k_rot = _rope(k_rope, cos_t, sin_lo, sin_hi)

    qna_ref[...] = (_bdot(hn, w_in_ref[:, 0:w]) * na_scale).astype(jnp.bfloat16)
    kna_ref[...] = _bdot(hn, w_in_ref[:, w:2 * w]).astype(jnp.bfloat16)
    vna_ref[...] = _bdot(hn, w_in_ref[:, 2 * w:3 * w]).astype(jnp.bfloat16)

    lane = lax.broadcasted_iota(jnp.int32, k_rot.shape, 1)
    nope = lane < MLA_QK_NOPE
    for h in range(MLA_HEADS):
        sl = slice(h * HEAD_SLOT, (h + 1) * HEAD_SLOT)
        qm_ref[:, sl] = (_rope(q[:, sl], cos_t, sin_lo, sin_hi) * mla_scale).astype(jnp.bfloat16)
        kr_ref[:, sl] = jnp.where(nope, kv[:, sl], k_rot).astype(jnp.bfloat16)
        v1_ref[:, sl] = jnp.where(nope, 1.0, kv[:, sl]).astype(jnp.bfloat16)


def _proj(h, pre_g, w_in, qn_g, w_uq, kvn_g, w_ukv, cos_t, sin_lo, sin_hi, *, tm=512):
    b, s, d = h.shape
    assert s % tm == 0
    tiles_per_seq = s // tm
    row = functools.partial(_token_rows, s, tm)
    full = _resident
    tab = pl.BlockSpec((tm, LANES), lambda i: (i % tiles_per_seq, 0))
    slots = MLA_HEADS * HEAD_SLOT
    bf = jnp.bfloat16
    return pl.pallas_call(
        functools.partial(_proj_kernel, na_scale=NA_HEAD_DIM ** -0.5 * LOG2_E,
                          mla_scale=(MLA_QK_NOPE + MLA_QK_ROPE) ** -0.5 * LOG2_E),
        grid=(b * s // tm,),
        in_specs=[row(d), full(pre_g), full(w_in), full(qn_g), full(w_uq), full(kvn_g),
                  full(w_ukv), tab, tab, tab],
        out_specs=[row(NA_WIDTH), row(NA_WIDTH), row(NA_WIDTH), row(slots), row(slots), row(slots)],
        out_shape=[jax.ShapeDtypeStruct((b, s, NA_WIDTH), bf)] * 3
        + [jax.ShapeDtypeStruct((b, s, slots), bf)] * 3,
        compiler_params=pltpu.CompilerParams(
            dimension_semantics=("arbitrary",), vmem_limit_bytes=VMEM_LIMIT),
        name="proj",
    )(h, pre_g, w_in, qn_g, w_uq, kvn_g, w_ukv, cos_t, sin_lo, sin_hi)


def _na_kernel(q_ref, k_ref, v_ref, bias_ref, o_ref, *, n_rows):
    g = pl.program_id(1)
    key_row0 = jnp.clip(g * NA_GROUP_ROWS - NA_WIN_H // 2, 0, n_rows - NA_KEY_ROWS)
    start = pl.multiple_of(key_row0 * GRID_W, GRID_W)
    n_keys = NA_KEY_ROWS * GRID_W
    lane = lax.broadcasted_iota(jnp.int32, (NA_GROUP_ROWS * GRID_W, LANES), 1)
    low = lane < NA_HEAD_DIM
    ones = jnp.ones((n_keys, LANES), jnp.bfloat16)
    for pair in range(NA_HEADS // 2):
        sl = slice(pair * LANES, (pair + 1) * LANES)
        q = q_ref[0, :, sl]
        k = k_ref[0, pl.ds(start, n_keys), sl]
        v1 = jnp.concatenate([v_ref[0, pl.ds(start, n_keys), sl], ones], axis=1)
        outs = []
        for e in range(2):
            qe = jnp.where(low if e == 0 else ~low, q, jnp.zeros_like(q))
            s = lax.dot_general(qe, k, _NT, preferred_element_type=jnp.float32)
            s = s + bias_ref[0, 2 * pair + e]
            p = jnp.exp2(s - jnp.max(s, axis=-1, keepdims=True))
            o = _bdot(p.astype(jnp.bfloat16), v1)
            outs.append(o[:, :LANES] / o[:, LANES:])
        o_ref[0, :, sl] = jnp.where(low, outs[0], outs[1]).astype(o_ref.dtype)


def _na_bias(rpb, n_rows):
    G, KR, W = NA_GROUP_ROWS, NA_KEY_ROWS, GRID_W
    n_groups = n_rows // G
    kh = min(NA_WIN_H, n_rows)
    qc = np.arange(W)[:, None]
    kc = np.arange(W)[None, :]
    win0 = np.clip(qc - NA_WIN_W // 2, 0, W - NA_WIN_W)
    col_ok = (kc >= win0) & (kc < win0 + NA_WIN_W)
    col_off = np.clip(kc - qc + NA_WIN_W - 1, 0, 2 * NA_WIN_W - 2)
    col_sel = jnp.asarray(np.eye(2 * NA_WIN_W - 1, dtype=np.float32)[:, col_off])
    variants = []
    for g in (0, 1, n_groups - 1):
        r = g * G + np.arange(G)[:, None]
        key_row0 = np.clip(g * G - NA_WIN_H // 2, 0, n_rows - KR)
        kr = key_row0 + np.arange(KR)[None, :]
        row0 = np.clip(r - kh // 2, 0, n_rows - kh)
        row_ok = (kr >= row0) & (kr < row0 + kh)
        row_off = np.clip(kr - r + NA_WIN_H - 1, 0, 2 * NA_WIN_H - 2)
        ok = row_ok[:, None, :, None] & col_ok[None, :, None, :]
        row_sel = jnp.asarray(np.eye(2 * NA_WIN_H - 1, dtype=np.float32)[row_off])
        b = jnp.einsum('gki,hij,jqc->hgqkc', row_sel, rpb, col_sel,
                       precision=lax.Precision.HIGHEST)
        b = jnp.where(ok[None], b * LOG2_E, NEG_INF)
        variants.append(b.reshape(rpb.shape[0], G * W, KR * W))
    return jnp.stack(variants)


def _na_attention(q, k, v, bias):
    b, s, w = q.shape
    n_rows = s // GRID_W
    assert n_rows % NA_GROUP_ROWS == 0 and n_rows >= NA_KEY_ROWS and n_rows // NA_GROUP_ROWS >= 3
    n_groups = n_rows // NA_GROUP_ROWS
    tq = NA_GROUP_ROWS * GRID_W

    def bias_map(bi, g):
        variant = jnp.where(g == 0, 0, jnp.where(g == n_groups - 1, 2, 1))
        return (variant, 0, 0, 0)

    whole = pl.BlockSpec((1, s, w), lambda bi, g: (bi, 0, 0))
    qblk = pl.BlockSpec((1, tq, w), lambda bi, g: (bi, g, 0))
    return pl.pallas_call(
        functools.partial(_na_kernel, n_rows=n_rows),
        grid=(b, n_groups),
        in_specs=[qblk, whole, whole, pl.BlockSpec((1,) + bias.shape[1:], bias_map)],
        out_specs=qblk,
        out_shape=jax.ShapeDtypeStruct((b, s, w), jnp.bfloat16),
        compiler_params=pltpu.CompilerParams(
            dimension_semantics=("arbitrary", "arbitrary"), vmem_limit_bytes=VMEM_LIMIT),
        name="na_attn",
    )(q, k, v, bias)


def _mla_kernel(q_ref, kr_ref, kv_ref, o_ref, s_ref, acc_ref, *, tk):
    n_chunks = kr_ref.shape[1] // tk
    tq = q_ref.shape[1]
    slots = [slice(e * HEAD_SLOT, (e + 1) * HEAD_SLOT) for e in range(2)]

    def score_chunk(c, m):
        keys = pl.ds(pl.multiple_of(c * tk, tk), tk)
        new_m = []
        for e, sl in enumerate(slots):
            s = lax.dot_general(q_ref[0, :, sl], kr_ref[0, keys, sl], _NT,
                                preferred_element_type=jnp.float32)
            s_ref[e, c] = s
            new_m.append(jnp.maximum(m[e], jnp.max(s, axis=-1, keepdims=True)))
        return tuple(new_m)

    m = lax.fori_loop(0, n_chunks, score_chunk,
                      tuple(jnp.full((tq, 1), -jnp.inf, jnp.float32) for _ in slots))

    acc_ref[...] = jnp.zeros(acc_ref.shape, jnp.float32)

    @pl.loop(0, n_chunks)
    def _(c):
        keys = pl.ds(pl.multiple_of(c * tk, tk), tk)
        for e, sl in enumerate(slots):
            p = jnp.exp2(s_ref[e, c] - m[e])
            acc_ref[e] += _bdot(p.astype(jnp.bfloat16), kv_ref[0, keys, sl])

    lane = lax.broadcasted_iota(jnp.int32, (tq, HEAD_SLOT), 1)
    acc = [acc_ref[e] for e in range(2)]
    swapped = [pltpu.roll(a, LANES - MLA_V_DIM, axis=1) for a in acc]
    o_ref[0] = jnp.where(lane < MLA_V_DIM, swapped[0] / acc[0], acc[1] / swapped[1]).astype(o_ref.dtype)


def _mla_attention(qm, kr, kv, *, tq=512, tk=4096):
    b, s, slots = qm.shape
    n_pairs = MLA_HEADS // 2
    assert s % tq == 0 and s % tk == 0 and slots == MLA_HEADS * HEAD_SLOT
    pair_w = 2 * HEAD_SLOT
    qblk = pl.BlockSpec((1, tq, pair_w), lambda bi, p, t: (bi, t, p))
    keys = pl.BlockSpec((1, s, pair_w), lambda bi, p, t: (bi, 0, p))
    return pl.pallas_call(
        functools.partial(_mla_kernel, tk=tk),
        grid=(b, n_pairs, s // tq),
        in_specs=[qblk, keys, keys],
        out_specs=pl.BlockSpec((1, tq, 2 * MLA_V_DIM), lambda bi, p, t: (bi, t, p)),
        out_shape=jax.ShapeDtypeStruct((b, s, MLA_WIDTH), jnp.bfloat16),
        scratch_shapes=[pltpu.VMEM((2, s // tk, tq, tk), jnp.float32),
                        pltpu.VMEM((2, tq, HEAD_SLOT), jnp.float32)],
        compiler_params=pltpu.CompilerParams(
            dimension_semantics=("arbitrary", "arbitrary", "arbitrary"),
            vmem_limit_bytes=VMEM_LIMIT),
        name="mla_attn",
    )(qm, kr, kv)


def _rope_tables(seq):
    t = jnp.arange(seq)
    row = (t // GRID_W).astype(jnp.float32)
    col = (t % GRID_W).astype(jnp.float32)
    n_freq = MLA_QK_ROPE // 4
    inv_freq = 1.0 / (ROPE_THETA ** (jnp.arange(n_freq, dtype=jnp.float32) / n_freq))
    ang = jnp.concatenate([row[:, None] * inv_freq[None, :], col[:, None] * inv_freq[None, :]], axis=-1)
    cos, sin = jnp.cos(ang), jnp.sin(ang)
    zeros = lambda width: jnp.zeros((seq, width), jnp.float32)
    tail = LANES - ROT_LANE0 - MLA_QK_ROPE
    cos_t = jnp.concatenate([jnp.ones((seq, ROT_LANE0), jnp.float32), cos, cos, zeros(tail)], axis=-1)
    sin_lo = jnp.concatenate([zeros(ROT_LANE0), -sin, zeros(HALF_ROT + tail)], axis=-1)
    sin_hi = jnp.concatenate([zeros(ROT_LANE0 + HALF_ROT), sin, zeros(tail)], axis=-1)
    return cos_t, sin_lo, sin_hi


def _pad_cols(w, left, right):
    return jnp.pad(w, ((0, 0), (left, right)))


def kernel(x, ffn1_pre_g, ffn1_w_gu, ffn1_w_down, ffn1_post_g, mix_pre_g, w_in, na_rpb, mla_q_norm_g, mla_w_uq, mla_kv_norm_g, mla_w_ukv, na_out_norm_g, mla_out_norm_g, w_out, mix_post_g, ffn2_pre_g, ffn2_w_gu, ffn2_w_down, ffn2_post_g):
    B, S, D = x.shape
    depth = ffn1_w_gu.shape[0]
    d_ff = ffn1_w_down.shape[1]
    bf = jnp.bfloat16
    n_rows = S // GRID_W
    cos_t, sin_lo, sin_hi = _rope_tables(S)
    row2 = lambda g: g.reshape(1, -1)

    def ffn_params(pre_g, w_gu, w_down, post_g):
        return (row2(pre_g), w_gu[:, :d_ff].astype(bf), w_gu[:, d_ff:].astype(bf), w_down.astype(bf),
                row2(post_g))

    h = x
    for l in range(depth):
        h = _ffn(h, ffn_params(ffn1_pre_g[l], ffn1_w_gu[l], ffn1_w_down[l], ffn1_post_g[l]))

        n_main = 3 * NA_WIDTH + MLA_Q_LORA + MLA_KV_LORA
        w_in_l = jnp.concatenate(
            [w_in[l, :, :n_main],
             _pad_cols(w_in[l, :, n_main:], ROT_LANE0, LANES - ROT_LANE0 - MLA_QK_ROPE)], axis=-1).astype(bf)
        qk = MLA_QK_NOPE + MLA_QK_ROPE
        w_uq_l = jnp.pad(mla_w_uq[l].reshape(MLA_Q_LORA, MLA_HEADS, qk),
                         ((0, 0), (0, 0), (0, HEAD_SLOT - qk))).reshape(MLA_Q_LORA, MLA_HEADS * HEAD_SLOT).astype(bf)
        qna, kna, vna, qm, kr, v1 = _proj(
            h, row2(mix_pre_g[l]), w_in_l, row2(mla_q_norm_g[l]), w_uq_l, row2(mla_kv_norm_g[l]),
            mla_w_ukv[l].astype(bf), cos_t, sin_lo, sin_hi)

        bias = _na_bias(na_rpb[l].astype(jnp.float32), n_rows)
        o_na = _na_attention(qna, kna, vna, bias)
        o_mla = _mla_attention(qm, kr, v1)

        mix = (o_na, o_mla, row2(na_out_norm_g[l]), row2(mla_out_norm_g[l]),
               w_out[l, :NA_WIDTH].astype(bf), w_out[l, NA_WIDTH:].astype(bf), row2(mix_post_g[l]))
        h = _ffn(h, ffn_params(ffn2_pre_g[l], ffn2_w_gu[l], ffn2_w_down[l], ffn2_post_g[l]), mix=mix)
    return h
```

```python
import functools

import numpy as np
import jax
import jax.numpy as jnp
from jax import lax
from jax.experimental import pallas as pl
from jax.experimental.pallas import tpu as pltpu

GRID_W = 64
EPS = 1e-6
NEG_INF = -1e30

NA_HEADS = 8
NA_HEAD_DIM = 64
NA_WIN_H = 8
NA_WIN_W = 16
NA_WIDTH = NA_HEADS * NA_HEAD_DIM

MLA_HEADS = 8
MLA_QK_NOPE = 64
MLA_QK_ROPE = 32
MLA_V_DIM = 64
MLA_Q_LORA = 256
MLA_KV_LORA = 128
MLA_WIDTH = MLA_HEADS * MLA_V_DIM
ROPE_THETA = 10000.0

MXU_WIDTH = 256
LANES = 128
HEAD_SLOT = LANES
ROT_LANE0 = MLA_QK_NOPE
HALF_ROT = MLA_QK_ROPE // 2

NA_GROUP_ROWS = 4
NA_KEY_ROWS = 2 * ((NA_GROUP_ROWS + NA_WIN_H) // 2)

VMEM_LIMIT = 56 * 1024 * 1024

_NT = (((1,), (1,)), ((), ()))
LOG2_E = float(np.log2(np.e))


def _rms(x, g):
    return x * lax.rsqrt(jnp.mean(x * x, axis=-1, keepdims=True) + EPS) * g


def _bdot(a, b):
    return jnp.dot(a, b, preferred_element_type=jnp.float32)


def _token_rows(seq, tm, width):
    tiles = seq // tm
    return pl.BlockSpec((None, tm, width), lambda i: (i // tiles, i % tiles, 0))


def _resident(a):
    return pl.BlockSpec(a.shape, lambda i: (0,) * a.ndim)


def _ffn_half_step(x, pre_g_ref, wg_ref, wu_ref, wd_ref, post_g_ref, ff_chunk):
    xn = _rms(x, pre_g_ref[...]).astype(jnp.bfloat16)
    d_ff = wg_ref.shape[1]
    acc = jnp.zeros(x.shape, jnp.float32)
    for c0 in range(0, d_ff, ff_chunk):
        c1 = min(c0 + ff_chunk, d_ff)
        gate = _bdot(xn, wg_ref[:, c0:c1])
        up = _bdot(xn, wu_ref[:, c0:c1])
        act = (gate * jax.nn.sigmoid(gate) * up).astype(jnp.bfloat16)
        acc = acc + _bdot(act, wd_ref[c0:c1, :])
    return x + 0.5 * _rms(acc, post_g_ref[...])


def _ffn_kernel(x_ref, pre_g_ref, wg_ref, wu_ref, wd_ref, post_g_ref, o_ref, *, ff_chunk):
    o_ref[...] = _ffn_half_step(x_ref[...], pre_g_ref, wg_ref, wu_ref, wd_ref, post_g_ref, ff_chunk)


def _mix_ffn_kernel(h_ref, ona_ref, omla_ref, na_g_ref, mla_g_ref, w_na_ref, w_mla_ref, mix_post_g_ref,
                    pre_g_ref, wg_ref, wu_ref, wd_ref, post_g_ref, o_ref, *, ff_chunk):
    na = _rms(ona_ref[...].astype(jnp.float32), na_g_ref[...]).astype(jnp.bfloat16)
    mla = _rms(omla_ref[...].astype(jnp.float32), mla_g_ref[...]).astype(jnp.bfloat16)
    mixed = _bdot(na, w_na_ref[...]) + _bdot(mla, w_mla_ref[...])
    h = h_ref[...] + _rms(mixed, mix_post_g_ref[...])
    o_ref[...] = _ffn_half_step(h, pre_g_ref, wg_ref, wu_ref, wd_ref, post_g_ref, ff_chunk)


def _ffn(x, ffn_params, mix=None, *, tm=512, ff_chunk=6 * MXU_WIDTH):
    b, s, d = x.shape
    d_ff = ffn_params[1].shape[1]
    assert s % tm == 0 and ff_chunk % MXU_WIDTH == 0 and d_ff % MXU_WIDTH == 0
    row = functools.partial(_token_rows, s, tm)
    if mix is None:
        body, operands, specs = _ffn_kernel, [x], [row(d)]
    else:
        body, operands = _mix_ffn_kernel, [x, *mix]
        specs = [row(d), row(mix[0].shape[-1]), row(mix[1].shape[-1])] + [_resident(a) for a in mix[2:]]
    return pl.pallas_call(
        functools.partial(body, ff_chunk=ff_chunk),
        grid=(b * s // tm,),
        in_specs=specs + [_resident(a) for a in ffn_params],
        out_specs=row(d),
        out_shape=jax.ShapeDtypeStruct((b, s, d), jnp.float32),
        compiler_params=pltpu.CompilerParams(
            dimension_semantics=("arbitrary",), vmem_limit_bytes=VMEM_LIMIT),
        name="ffn" if mix is None else "mix_ffn",
    )(*operands, *ffn_params)


def _rope(x, cos_t, sin_lo, sin_hi):
    from_hi = pltpu.roll(x, LANES - HALF_ROT, axis=1)
    from_lo = pltpu.roll(x, HALF_ROT, axis=1)
    return x * cos_t + from_hi * sin_lo + from_lo * sin_hi


def _proj_kernel(h_ref, pre_g_ref, w_in_ref, qn_g_ref, w_uq_ref, kvn_g_ref, w_ukv_ref,
                 cos_ref, sin_lo_ref, sin_hi_ref,
                 qna_ref, kna_ref, vna_ref, qm_ref, kr_ref, v1_ref, *, na_scale, mla_scale):
    hn = _rms(h_ref[...], pre_g_ref[...]).astype(jnp.bfloat16)
    w = NA_WIDTH
    z_lat = _bdot(hn, w_in_ref[:, 3 * w:])
    c_q = z_lat[:, :MLA_Q_LORA]
    c_kv = z_lat[:, MLA_Q_LORA:MLA_Q_LORA + MLA_KV_LORA]
    k_rope = z_lat[:, MLA_Q_LORA + MLA_KV_LORA:]

    cos_t, sin_lo, sin_hi = cos_ref[...], sin_lo_ref[...], sin_hi_ref[...]
    q = _bdot(_rms(c_q, qn_g_ref[...]).astype(jnp.bfloat16), w_uq_ref[...])
    kv = _bdot(_rms(c_kv, kvn_g_ref[...]).astype(jnp.bfloat16), w_ukv_ref[...])
    k_rot = _rope(k_rope, cos_t, sin_lo, sin_hi)

    qna_ref[...] = (_bdot(hn, w_in_ref[:, 0:w]) * na_scale).astype(jnp.bfloat16)
    kna_ref[...] = _bdot(hn, w_in_ref[:, w:2 * w]).astype(jnp.bfloat16)
    vna_ref[...] = _bdot(hn, w_in_ref[:, 2 * w:3 * w]).astype(jnp.bfloat16)

    lane = lax.broadcasted_iota(jnp.int32, k_rot.shape, 1)
    nope = lane < MLA_QK_NOPE
    for h in range(MLA_HEADS):
        sl = slice(h * HEAD_SLOT, (h + 1) * HEAD_SLOT)
        qm_ref[:, sl] = (_rope(q[:, sl], cos_t, sin_lo, sin_hi) * mla_scale).astype(jnp.bfloat16)
        kr_ref[:, sl] = jnp.where(nope, kv[:, sl], k_rot).astype(jnp.bfloat16)
        v1_ref[:, sl] = jnp.where(nope, 1.0, kv[:, sl]).astype(jnp.bfloat16)


def _proj(h, pre_g, w_in, qn_g, w_uq, kvn_g, w_ukv, cos_t, sin_lo, sin_hi, *, tm=512):
    b, s, d = h.shape
    assert s % tm == 0
    tiles_per_seq = s // tm
    row = functools.partial(_token_rows, s, tm)
    full = _resident
    tab = pl.BlockSpec((tm, LANES), lambda i: (i % tiles_per_seq, 0))
    slots = MLA_HEADS * HEAD_SLOT
    bf = jnp.bfloat16
    return pl.pallas_call(
        functools.partial(_proj_kernel, na_scale=NA_HEAD_DIM ** -0.5 * LOG2_E,
                          mla_scale=(MLA_QK_NOPE + MLA_QK_ROPE) ** -0.5 * LOG2_E),
        grid=(b * s // tm,),
        in_specs=[row(d), full(pre_g), full(w_in), full(qn_g), full(w_uq), full(kvn_g),
                  full(w_ukv), tab, tab, tab],
        out_specs=[row(NA_WIDTH), row(NA_WIDTH), row(NA_WIDTH), row(slots), row(slots), row(slots)],
        out_shape=[jax.ShapeDtypeStruct((b, s, NA_WIDTH), bf)] * 3
        + [jax.ShapeDtypeStruct((b, s, slots), bf)] * 3,
        compiler_params=pltpu.CompilerParams(
            dimension_semantics=("arbitrary",), vmem_limit_bytes=VMEM_LIMIT),
        name="proj",
    )(h, pre_g, w_in, qn_g, w_uq, kvn_g, w_ukv, cos_t, sin_lo, sin_hi)


def _na_kernel(q_ref, k_ref, v_ref, bias_ref, o_ref, *, n_rows):
    g = pl.program_id(1)
    key_row0 = jnp.clip(g * NA_GROUP_ROWS - NA_WIN_H // 2, 0, n_rows - NA_KEY_ROWS)
    start = pl.multiple_of(key_row0 * GRID_W, GRID_W)
    n_keys = NA_KEY_ROWS * GRID_W
    lane = lax.broadcasted_iota(jnp.int32, (NA_GROUP_ROWS * GRID_W, LANES), 1)
    low = lane < NA_HEAD_DIM
    ones = jnp.ones((n_keys, LANES), jnp.bfloat16)
    for pair in range(NA_HEADS // 2):
        sl = slice(pair * LANES, (pair + 1) * LANES)
        q = q_ref[0, :, sl]
        k = k_ref[0, pl.ds(start, n_keys), sl]
        v1 = jnp.concatenate([v_ref[0, pl.ds(start, n_keys), sl], ones], axis=1)
        outs = []
        for e in range(2):
            qe = jnp.where(low if e == 0 else ~low, q, jnp.zeros_like(q))
            s = lax.dot_general(qe, k, _NT, preferred_element_type=jnp.float32)
            s = s + jnp.concatenate([bias_ref[0, 2 * pair + e, j] for j in range(NA_KEY_ROWS // 2)], axis=1)
            p = jnp.exp2(s - jnp.max(s, axis=-1, keepdims=True))
            o = _bdot(p.astype(jnp.bfloat16), v1)
            outs.append(o[:, :LANES] / o[:, LANES:])
        o_ref[0, :, sl] = jnp.where(low, outs[0], outs[1]).astype(o_ref.dtype)


def _na_bias(rpb, n_rows):
    G, KR, W = NA_GROUP_ROWS, NA_KEY_ROWS, GRID_W
    n_groups = n_rows // G
    n_pairs = KR // 2
    kh = min(NA_WIN_H, n_rows)
    n_ro, n_co = 2 * NA_WIN_H - 1, 2 * NA_WIN_W - 1
    qc = np.arange(W)[:, None]
    kc = np.arange(W)[None, :]
    win0 = np.clip(qc - NA_WIN_W // 2, 0, W - NA_WIN_W)
    col_ok = (kc >= win0) & (kc < win0 + NA_WIN_W)
    col_off = np.clip(kc - qc + NA_WIN_W - 1, 0, n_co - 1)
    col_sel = np.zeros((2, n_co, W, 2, W), np.float32)
    for half in range(2):
        col_sel[half, :, :, half, :] = np.eye(n_co, dtype=np.float32)[:, col_off]
    col_sel = jnp.asarray(col_sel.reshape(2 * n_co, W, 2 * W))
    rpb_pad = jnp.pad(rpb, ((0, 0), (1, 1), (0, 0)))
    first_second = jnp.concatenate([rpb_pad[:, :-1], rpb_pad[:, 1:]], axis=-1)
    pair_tab = jnp.einsum('hik,kql->hiql', first_second, col_sel, precision=lax.Precision.HIGHEST)

    sels, oks = [], []
    for g in (0, 1, n_groups - 1):
        r = g * G + np.arange(G)[:, None]
        key_row0 = np.clip(g * G - NA_WIN_H // 2, 0, n_rows - KR)
        kr = key_row0 + np.arange(KR)[None, :]
        row0 = np.clip(r - kh // 2, 0, n_rows - kh)
        row_ok = (kr >= row0) & (kr < row0 + kh)
        first_off = np.clip(kr[:, 0::2] - r + NA_WIN_H - 1, -1, n_ro - 1)
        sels.append(np.eye(n_ro + 1, dtype=np.float32)[first_off + 1])
        ok = row_ok.reshape(G, 1, n_pairs, 2, 1) & col_ok.reshape(1, W, 1, 1, W)
        oks.append(ok.reshape(G, W, n_pairs, 2 * W).transpose(2, 0, 1, 3))
    sel = jnp.asarray(np.stack(sels))
    ok = np.stack(oks)[:, None]
    b = jnp.einsum('vgji,hiql->vhjgql', sel, pair_tab, precision=lax.Precision.HIGHEST)
    b = jnp.where(ok, b * LOG2_E, NEG_INF)
    return b.reshape(3, rpb.shape[0], n_pairs, G * W, 2 * W)


def _na_attention(q, k, v, bias):
    b, s, w = q.shape
    n_rows = s // GRID_W
    assert n_rows % NA_GROUP_ROWS == 0 and n_rows >= NA_KEY_ROWS and n_rows // NA_GROUP_ROWS >= 3
    n_groups = n_rows // NA_GROUP_ROWS
    tq = NA_GROUP_ROWS * GRID_W

    def bias_map(bi, g):
        variant = jnp.where(g == 0, 0, jnp.where(g == n_groups - 1, 2, 1))
        return (variant, 0, 0, 0, 0)

    whole = pl.BlockSpec((1, s, w), lambda bi, g: (bi, 0, 0))
    qblk = pl.BlockSpec((1, tq, w), lambda bi, g: (bi, g, 0))
    return pl.pallas_call(
        functools.partial(_na_kernel, n_rows=n_rows),
        grid=(b, n_groups),
        in_specs=[qblk, whole, whole, pl.BlockSpec((1,) + bias.shape[1:], bias_map)],
        out_specs=qblk,
        out_shape=jax.ShapeDtypeStruct((b, s, w), jnp.bfloat16),
        compiler_params=pltpu.CompilerParams(
            dimension_semantics=("arbitrary", "arbitrary"), vmem_limit_bytes=VMEM_LIMIT),
        name="na_attn",
    )(q, k, v, bias)


def _mla_kernel(q_ref, kr_ref, kv_ref, o_ref, s_ref, acc_ref, *, tk):
    n_chunks = kr_ref.shape[1] // tk
    tq = q_ref.shape[1]
    slots = [slice(e * HEAD_SLOT, (e + 1) * HEAD_SLOT) for e in range(2)]

    def score_chunk(c, m):
        keys = pl.ds(pl.multiple_of(c * tk, tk), tk)
        new_m = []
        for e, sl in enumerate(slots):
            s = lax.dot_general(q_ref[0, :, sl], kr_ref[0, keys, sl], _NT,
                                preferred_element_type=jnp.float32)
            s_ref[e, c] = s
            new_m.append(jnp.maximum(m[e], jnp.max(s, axis=-1, keepdims=True)))
        return tuple(new_m)

    m = lax.fori_loop(0, n_chunks, score_chunk,
                      tuple(jnp.full((tq, 1), -jnp.inf, jnp.float32) for _ in slots))

    acc_ref[...] = jnp.zeros(acc_ref.shape, jnp.float32)

    @pl.loop(0, n_chunks)
    def _(c):
        keys = pl.ds(pl.multiple_of(c * tk, tk), tk)
        for e, sl in enumerate(slots):
            p = jnp.exp2(s_ref[e, c] - m[e])
            acc_ref[e] += _bdot(p.astype(jnp.bfloat16), kv_ref[0, keys, sl])

    lane = lax.broadcasted_iota(jnp.int32, (tq, HEAD_SLOT), 1)
    acc = [acc_ref[e] for e in range(2)]
    swapped = [pltpu.roll(a, LANES - MLA_V_DIM, axis=1) for a in acc]
    o_ref[0] = jnp.where(lane < MLA_V_DIM, swapped[0] / acc[0], acc[1] / swapped[1]).astype(o_ref.dtype)


def _mla_attention(qm, kr, kv, *, tq=512, tk=4096):
    b, s, slots = qm.shape
    n_pairs = MLA_HEADS // 2
    assert s % tq == 0 and s % tk == 0 and slots == MLA_HEADS * HEAD_SLOT
    pair_w = 2 * HEAD_SLOT
    qblk = pl.BlockSpec((1, tq, pair_w), lambda bi, p, t: (bi, t, p))
    keys = pl.BlockSpec((1, s, pair_w), lambda bi, p, t: (bi, 0, p))
    return pl.pallas_call(
        functools.partial(_mla_kernel, tk=tk),
        grid=(b, n_pairs, s // tq),
        in_specs=[qblk, keys, keys],
        out_specs=pl.BlockSpec((1, tq, 2 * MLA_V_DIM), lambda bi, p, t: (bi, t, p)),
        out_shape=jax.ShapeDtypeStruct((b, s, MLA_WIDTH), jnp.bfloat16),
        scratch_shapes=[pltpu.VMEM((2, s // tk, tq, tk), jnp.float32),
                        pltpu.VMEM((2, tq, HEAD_SLOT), jnp.float32)],
        compiler_params=pltpu.CompilerParams(
            dimension_semantics=("arbitrary", "arbitrary", "arbitrary"),
            vmem_limit_bytes=VMEM_LIMIT),
        name="mla_attn",
    )(qm, kr, kv)


def _rope_tables(seq):
    t = jnp.arange(seq)
    row = (t // GRID_W).astype(jnp.float32)
    col = (t % GRID_W).astype(jnp.float32)
    n_freq = MLA_QK_ROPE // 4
    inv_freq = 1.0 / (ROPE_THETA ** (jnp.arange(n_freq, dtype=jnp.float32) / n_freq))
    ang = jnp.concatenate([row[:, None] * inv_freq[None, :], col[:, None] * inv_freq[None, :]], axis=-1)
    cos, sin = jnp.cos(ang), jnp.sin(ang)
    zeros = lambda width: jnp.zeros((seq, width), jnp.float32)
    tail = LANES - ROT_LANE0 - MLA_QK_ROPE
    cos_t = jnp.concatenate([jnp.ones((seq, ROT_LANE0), jnp.float32), cos, cos, zeros(tail)], axis=-1)
    sin_lo = jnp.concatenate([zeros(ROT_LANE0), -sin, zeros(HALF_ROT + tail)], axis=-1)
    sin_hi = jnp.concatenate([zeros(ROT_LANE0 + HALF_ROT), sin, zeros(tail)], axis=-1)
    return cos_t, sin_lo, sin_hi


def _pad_cols(w, left, right):
    return jnp.pad(w, ((0, 0), (left, right)))


def kernel(x, ffn1_pre_g, ffn1_w_gu, ffn1_w_down, ffn1_post_g, mix_pre_g, w_in, na_rpb, mla_q_norm_g, mla_w_uq, mla_kv_norm_g, mla_w_ukv, na_out_norm_g, mla_out_norm_g, w_out, mix_post_g, ffn2_pre_g, ffn2_w_gu, ffn2_w_down, ffn2_post_g):
    B, S, D = x.shape
    depth = ffn1_w_gu.shape[0]
    d_ff = ffn1_w_down.shape[1]
    bf = jnp.bfloat16
    n_rows = S // GRID_W
    cos_t, sin_lo, sin_hi = _rope_tables(S)
    row2 = lambda g: g.reshape(1, -1)

    def ffn_params(pre_g, w_gu, w_down, post_g):
        return (row2(pre_g), w_gu[:, :d_ff].astype(bf), w_gu[:, d_ff:].astype(bf), w_down.astype(bf),
                row2(post_g))

    h = x
    for l in range(depth):
        h = _ffn(h, ffn_params(ffn1_pre_g[l], ffn1_w_gu[l], ffn1_w_down[l], ffn1_post_g[l]))

        n_main = 3 * NA_WIDTH + MLA_Q_LORA + MLA_KV_LORA
        w_in_l = jnp.concatenate(
            [w_in[l, :, :n_main],
             _pad_cols(w_in[l, :, n_main:], ROT_LANE0, LANES - ROT_LANE0 - MLA_QK_ROPE)], axis=-1).astype(bf)
        qk = MLA_QK_NOPE + MLA_QK_ROPE
        w_uq_l = jnp.pad(mla_w_uq[l].reshape(MLA_Q_LORA, MLA_HEADS, qk),
                         ((0, 0), (0, 0), (0, HEAD_SLOT - qk))).reshape(MLA_Q_LORA, MLA_HEADS * HEAD_SLOT).astype(bf)
        qna, kna, vna, qm, kr, v1 = _proj(
            h, row2(mix_pre_g[l]), w_in_l, row2(mla_q_norm_g[l]), w_uq_l, row2(mla_kv_norm_g[l]),
            mla_w_ukv[l].astype(bf), cos_t, sin_lo, sin_hi)

        bias = _na_bias(na_rpb[l].astype(jnp.float32), n_rows)
        o_na = _na_attention(qna, kna, vna, bias)
        o_mla = _mla_attention(qm, kr, v1)

        mix = (o_na, o_mla, row2(na_out_norm_g[l]), row2(mla_out_norm_g[l]),
               w_out[l, :NA_WIDTH].astype(bf), w_out[l, NA_WIDTH:].astype(bf), row2(mix_post_g[l]))
        h = _ffn(h, ffn_params(ffn2_pre_g[l], ffn2_w_gu[l], ffn2_w_down[l], ffn2_post_g[l]), mix=mix)
    return h
```

```python
import functools

import numpy as np
import jax
import jax.numpy as jnp
from jax import lax
from jax.experimental import pallas as pl
from jax.experimental.pallas import tpu as pltpu

GRID_W = 64
EPS = 1e-6
NEG_INF = -1e30

NA_HEADS = 8
NA_HEAD_DIM = 64
NA_WIN_H = 8
NA_WIN_W = 16
NA_WIDTH = NA_HEADS * NA_HEAD_DIM

MLA_HEADS = 8
MLA_QK_NOPE = 64
MLA_QK_ROPE = 32
MLA_V_DIM = 64
MLA_Q_LORA = 256
MLA_KV_LORA = 128
MLA_WIDTH = MLA_HEADS * MLA_V_DIM
ROPE_THETA = 10000.0

MXU_WIDTH = 256
LANES = 128
HEAD_SLOT = LANES
ROT_LANE0 = MLA_QK_NOPE
HALF_ROT = MLA_QK_ROPE // 2

NA_GROUP_ROWS = 4
NA_KEY_ROWS = 2 * ((NA_GROUP_ROWS + NA_WIN_H) // 2)

VMEM_LIMIT = 56 * 1024 * 1024

_NT = (((1,), (1,)), ((), ()))
LOG2_E = float(np.log2(np.e))


def _rms(x, g):
    return x * lax.rsqrt(jnp.mean(x * x, axis=-1, keepdims=True) + EPS) * g


def _bdot(a, b):
    return jnp.dot(a, b, preferred_element_type=jnp.float32)


def _token_rows(seq, tm, width):
    tiles = seq // tm
    return pl.BlockSpec((None, tm, width), lambda i: (i // tiles, i % tiles, 0))


def _resident(a):
    return pl.BlockSpec(a.shape, lambda i: (0,) * a.ndim)


def _swiglu(xn, wg_ref, wu_ref, wd_ref, ff_chunk):
    d_ff = wg_ref.shape[1]
    acc = jnp.zeros((xn.shape[0], wd_ref.shape[1]), jnp.float32)
    for c0 in range(0, d_ff, ff_chunk):
        c1 = min(c0 + ff_chunk, d_ff)
        gate = _bdot(xn, wg_ref[:, c0:c1])
        up = _bdot(xn, wu_ref[:, c0:c1])
        act = (gate * jax.nn.sigmoid(gate) * up).astype(jnp.bfloat16)
        acc = acc + _bdot(act, wd_ref[c0:c1, :])
    return acc


def _ffn_half_steps(xs, pre_g_ref, wg_ref, wu_ref, wd_ref, post_g_ref, ff_chunk):
    xns = [_rms(x, pre_g_ref[...]).astype(jnp.bfloat16) for x in xs]
    accs = [_swiglu(xn, wg_ref, wu_ref, wd_ref, ff_chunk) for xn in xns]
    return [x + 0.5 * _rms(acc, post_g_ref[...]) for x, acc in zip(xs, accs)]


def _sub_tiles(n_rows, sub_rows):
    return [slice(r0, r0 + sub_rows) for r0 in range(0, n_rows, sub_rows)]


def _ffn_kernel(x_ref, pre_g_ref, wg_ref, wu_ref, wd_ref, post_g_ref, o_ref, *, ff_chunk, sub_rows):
    tiles = _sub_tiles(x_ref.shape[0], sub_rows)
    outs = _ffn_half_steps([x_ref[rows] for rows in tiles], pre_g_ref, wg_ref, wu_ref, wd_ref,
                           post_g_ref, ff_chunk)
    for rows, out in zip(tiles, outs):
        o_ref[rows] = out


def _mix_ffn_kernel(h_ref, ona_ref, omla_ref, na_g_ref, mla_g_ref, w_na_ref, w_mla_ref, mix_post_g_ref,
                    pre_g_ref, wg_ref, wu_ref, wd_ref, post_g_ref, o_ref, *, ff_chunk, sub_rows):
    tiles = _sub_tiles(h_ref.shape[0], sub_rows)
    hs = []
    for rows in tiles:
        na = _rms(ona_ref[rows].astype(jnp.float32), na_g_ref[...]).astype(jnp.bfloat16)
        mla = _rms(omla_ref[rows].astype(jnp.float32), mla_g_ref[...]).astype(jnp.bfloat16)
        mixed = _bdot(na, w_na_ref[...]) + _bdot(mla, w_mla_ref[...])
        hs.append(h_ref[rows] + _rms(mixed, mix_post_g_ref[...]))
    outs = _ffn_half_steps(hs, pre_g_ref, wg_ref, wu_ref, wd_ref, post_g_ref, ff_chunk)
    for rows, out in zip(tiles, outs):
        o_ref[rows] = out


def _ffn(x, ffn_params, mix=None, *, tm=1024, sub_rows=256, ff_chunk=6 * MXU_WIDTH):
    b, s, d = x.shape
    d_ff = ffn_params[1].shape[1]
    assert s % tm == 0 and tm % sub_rows == 0 and ff_chunk % MXU_WIDTH == 0 and d_ff % MXU_WIDTH == 0
    row = functools.partial(_token_rows, s, tm)
    if mix is None:
        body, operands, specs = _ffn_kernel, [x], [row(d)]
    else:
        body, operands = _mix_ffn_kernel, [x, *mix]
        specs = [row(d), row(mix[0].shape[-1]), row(mix[1].shape[-1])] + [_resident(a) for a in mix[2:]]
    return pl.pallas_call(
        functools.partial(body, ff_chunk=ff_chunk, sub_rows=sub_rows),
        grid=(b * s // tm,),
        in_specs=specs + [_resident(a) for a in ffn_params],
        out_specs=row(d),
        out_shape=jax.ShapeDtypeStruct((b, s, d), jnp.float32),
        compiler_params=pltpu.CompilerParams(
            dimension_semantics=("arbitrary",), vmem_limit_bytes=VMEM_LIMIT),
        name="ffn" if mix is None else "mix_ffn",
    )(*operands, *ffn_params)


def _rope(x, cos_t, sin_lo, sin_hi):
    from_hi = pltpu.roll(x, LANES - HALF_ROT, axis=1)
    from_lo = pltpu.roll(x, HALF_ROT, axis=1)
    return x * cos_t + from_hi * sin_lo + from_lo * sin_hi


def _proj_kernel(h_ref, pre_g_ref, w_in_ref, qn_g_ref, w_uq_ref, kvn_g_ref, w_ukv_ref,
                 cos_ref, sin_lo_ref, sin_hi_ref,
                 qna_ref, kna_ref, vna_ref, qm_ref, kr_ref, v1_ref, *, na_scale, mla_scale, sub_rows):
    w = NA_WIDTH
    tiles = _sub_tiles(h_ref.shape[0], sub_rows)
    hns = [_rms(h_ref[rows], pre_g_ref[...]).astype(jnp.bfloat16) for rows in tiles]
    z_lats = [_bdot(hn, w_in_ref[:, 3 * w:]) for hn in hns]
    latents = []
    for rows, z_lat in zip(tiles, z_lats):
        c_q = z_lat[:, :MLA_Q_LORA]
        c_kv = z_lat[:, MLA_Q_LORA:MLA_Q_LORA + MLA_KV_LORA]
        k_rope = z_lat[:, MLA_Q_LORA + MLA_KV_LORA:]
        tabs = (cos_ref[rows], sin_lo_ref[rows], sin_hi_ref[rows])
        q = _bdot(_rms(c_q, qn_g_ref[...]).astype(jnp.bfloat16), w_uq_ref[...])
        kv = _bdot(_rms(c_kv, kvn_g_ref[...]).astype(jnp.bfloat16), w_ukv_ref[...])
        latents.append((q, kv, _rope(k_rope, *tabs), tabs))

    for rows, hn in zip(tiles, hns):
        qna_ref[rows] = (_bdot(hn, w_in_ref[:, 0:w]) * na_scale).astype(jnp.bfloat16)
        kna_ref[rows] = _bdot(hn, w_in_ref[:, w:2 * w]).astype(jnp.bfloat16)
        vna_ref[rows] = _bdot(hn, w_in_ref[:, 2 * w:3 * w]).astype(jnp.bfloat16)

    lane = lax.broadcasted_iota(jnp.int32, (sub_rows, HEAD_SLOT), 1)
    nope = lane < MLA_QK_NOPE
    for rows, (q, kv, k_rot, tabs) in zip(tiles, latents):
        for h in range(MLA_HEADS):
            sl = slice(h * HEAD_SLOT, (h + 1) * HEAD_SLOT)
            qm_ref[rows, sl] = (_rope(q[:, sl], *tabs) * mla_scale).astype(jnp.bfloat16)
            kr_ref[rows, sl] = jnp.where(nope, kv[:, sl], k_rot).astype(jnp.bfloat16)
            v1_ref[rows, sl] = jnp.where(nope, 1.0, kv[:, sl]).astype(jnp.bfloat16)


def _proj(h, pre_g, w_in, qn_g, w_uq, kvn_g, w_ukv, cos_t, sin_lo, sin_hi, *, tm=1024, sub_rows=256):
    b, s, d = h.shape
    assert s % tm == 0 and tm % sub_rows == 0
    tiles_per_seq = s // tm
    row = functools.partial(_token_rows, s, tm)
    full = _resident
    tab =pl.BlockSpec((tm, LANES), lambda i: (i % tiles_per_seq, 0))
    slots = MLA_HEADS * HEAD_SLOT
    bf = jnp.bfloat16
    return pl.pallas_call(
        functools.partial(_proj_kernel, na_scale=NA_HEAD_DIM ** -0.5 * LOG2_E,
                          mla_scale=(MLA_QK_NOPE + MLA_QK_ROPE) ** -0.5 * LOG2_E, sub_rows=sub_rows),
        grid=(b * s // tm,),
        in_specs=[row(d), full(pre_g), full(w_in), full(qn_g), full(w_uq), full(kvn_g),
                  full(w_ukv), tab, tab, tab],
        out_specs=[row(NA_WIDTH), row(NA_WIDTH), row(NA_WIDTH), row(slots), row(slots), row(slots)],
        out_shape=[jax.ShapeDtypeStruct((b, s, NA_WIDTH), bf)] * 3
        + [jax.ShapeDtypeStruct((b, s, slots), bf)] * 3,
        compiler_params=pltpu.CompilerParams(
            dimension_semantics=("arbitrary",), vmem_limit_bytes=VMEM_LIMIT),
        name="proj",
    )(h, pre_g, w_in, qn_g, w_uq, kvn_g, w_ukv, cos_t, sin_lo, sin_hi)


def _na_kernel(q_ref, k_ref, v_ref, bias_ref, o_ref, *, n_rows):
    g = pl.program_id(1)
    key_row0 = jnp.clip(g * NA_GROUP_ROWS - NA_WIN_H // 2, 0, n_rows - NA_KEY_ROWS)
    start = pl.multiple_of(key_row0 * GRID_W, GRID_W)
    n_keys = NA_KEY_ROWS * GRID_W
    lane = lax.broadcasted_iota(jnp.int32, (NA_GROUP_ROWS * GRID_W, LANES), 1)
    low = lane < NA_HEAD_DIM
    ones = jnp.ones((n_keys, LANES), jnp.bfloat16)
    for pair in range(NA_HEADS // 2):
        sl = slice(pair * LANES, (pair + 1) * LANES)
        q = q_ref[0, :, sl]
        k = k_ref[0, pl.ds(start, n_keys), sl]
        v1 = jnp.concatenate([v_ref[0, pl.ds(start, n_keys), sl], ones], axis=1)
        outs = []
        for e in range(2):
            qe = jnp.where(low if e == 0 else ~low, q, jnp.zeros_like(q))
            s = lax.dot_general(qe, k, _NT, preferred_element_type=jnp.float32)
            s = s + jnp.concatenate([bias_ref[0, 2 * pair + e, j] for j in range(NA_KEY_ROWS // 2)], axis=1)
            p = jnp.exp2(s - jnp.max(s, axis=-1, keepdims=True))
            o = _bdot(p.astype(jnp.bfloat16), v1)
            outs.append(o[:, :LANES] / o[:, LANES:])
        o_ref[0, :, sl] = jnp.where(low, outs[0], outs[1]).astype(o_ref.dtype)


def _na_bias(rpb, n_rows):
    G, KR, W = NA_GROUP_ROWS, NA_KEY_ROWS, GRID_W
    n_groups = n_rows // G
    n_pairs = KR // 2
    kh = min(NA_WIN_H, n_rows)
    n_ro, n_co = 2 * NA_WIN_H - 1, 2 * NA_WIN_W - 1
    qc = np.arange(W)[:, None]
    kc = np.arange(W)[None, :]
    win0 = np.clip(qc - NA_WIN_W // 2, 0, W - NA_WIN_W)
    col_ok = (kc >= win0) & (kc < win0 + NA_WIN_W)
    col_off = np.clip(kc - qc + NA_WIN_W - 1, 0, n_co - 1)
    col_sel = np.zeros((2, n_co, W, 2, W), np.float32)
    for half in range(2):
        col_sel[half, :, :, half, :] = np.eye(n_co, dtype=np.float32)[:, col_off]
    col_sel = jnp.asarray(col_sel.reshape(2 * n_co, W, 2 * W))
    rpb_pad = jnp.pad(rpb, ((0, 0), (1, 1), (0, 0)))
    first_second = jnp.concatenate([rpb_pad[:, :-1], rpb_pad[:, 1:]], axis=-1)
    pair_tab = jnp.einsum('hik,kql->hiql', first_second, col_sel, precision=lax.Precision.HIGHEST)

    sels, oks = [], []
    for g in (0, 1, n_groups - 1):
        r = g * G + np.arange(G)[:, None]
        key_row0 = np.clip(g * G - NA_WIN_H // 2, 0, n_rows - KR)
        kr = key_row0 + np.arange(KR)[None, :]
        row0 = np.clip(r - kh // 2, 0, n_rows - kh)
        row_ok = (kr >= row0) & (kr < row0 + kh)
        first_off = np.clip(kr[:, 0::2] - r + NA_WIN_H - 1, -1, n_ro - 1)
        sels.append(np.eye(n_ro + 1, dtype=np.float32)[first_off + 1])
        ok = row_ok.reshape(G, 1, n_pairs, 2, 1) & col_ok.reshape(1, W, 1, 1, W)
        oks.append(ok.reshape(G, W, n_pairs, 2 * W).transpose(2, 0, 1, 3))
    sel = jnp.asarray(np.stack(sels))
    ok = np.stack(oks)[:, None]
    b = jnp.einsum('vgji,hiql->vhjgql', sel, pair_tab, precision=lax.Precision.HIGHEST)
    b = jnp.where(ok, b * LOG2_E, NEG_INF)
    return b.reshape(3, rpb.shape[0], n_pairs, G * W, 2 * W)


def _na_attention(q, k, v, bias):
    b, s, w = q.shape
    n_rows = s // GRID_W
    assert n_rows % NA_GROUP_ROWS == 0 and n_rows >= NA_KEY_ROWS and n_rows // NA_GROUP_ROWS >= 3
    n_groups = n_rows // NA_GROUP_ROWS
    tq = NA_GROUP_ROWS * GRID_W

    def bias_map(bi, g):
        variant = jnp.where(g == 0, 0, jnp.where(g == n_groups - 1, 2, 1))
        return (variant, 0, 0, 0, 0)

    whole = pl.BlockSpec((1, s, w), lambda bi, g: (bi, 0, 0))
    qblk = pl.BlockSpec((1, tq, w), lambda bi, g: (bi, g, 0))
    return pl.pallas_call(
        functools.partial(_na_kernel, n_rows=n_rows),
        grid=(b, n_groups),
        in_specs=[qblk, whole, whole, pl.BlockSpec((1,) + bias.shape[1:], bias_map)],
        out_specs=qblk,
        out_shape=jax.ShapeDtypeStruct((b, s, w), jnp.bfloat16),
        compiler_params=pltpu.CompilerParams(
            dimension_semantics=("arbitrary", "arbitrary"), vmem_limit_bytes=VMEM_LIMIT),
        name="na_attn",
    )(q, k, v, bias)


def _mla_kernel(q_ref, kr_ref, kv_ref, o_ref, s_ref, acc_ref, *, tk):
    n_chunks = kr_ref.shape[1] // tk
    tq = q_ref.shape[1]
    slots = [slice(e * HEAD_SLOT, (e + 1) * HEAD_SLOT) for e in range(2)]

    def score_chunk(c, m):
        keys = pl.ds(pl.multiple_of(c * tk, tk), tk)
        new_m = []
        for e, sl in enumerate(slots):
            s = lax.dot_general(q_ref[0, :, sl], kr_ref[0, keys, sl], _NT,
                                preferred_element_type=jnp.float32)
            s_ref[e, c] = s
            new_m.append(jnp.maximum(m[e], jnp.max(s, axis=-1, keepdims=True)))
        return tuple(new_m)

    m = lax.fori_loop(0, n_chunks, score_chunk,
                      tuple(jnp.full((tq, 1), -jnp.inf, jnp.float32) for _ in slots))

    acc_ref[...] = jnp.zeros(acc_ref.shape, jnp.float32)

    @pl.loop(0, n_chunks)
    def _(c):
        keys = pl.ds(pl.multiple_of(c * tk, tk), tk)
        for e, sl in enumerate(slots):
            p = jnp.exp2(s_ref[e, c] - m[e])
            acc_ref[e] += _bdot(p.astype(jnp.bfloat16), kv_ref[0, keys, sl])

    lane = lax.broadcasted_iota(jnp.int32, (tq, HEAD_SLOT), 1)
    acc = [acc_ref[e] for e in range(2)]
    swapped = [pltpu.roll(a, LANES - MLA_V_DIM, axis=1) for a in acc]
    o_ref[0] = jnp.where(lane < MLA_V_DIM, swapped[0] / acc[0], acc[1] / swapped[1]).astype(o_ref.dtype)


def _mla_attention(qm, kr, kv, *, tq=512, tk=4096):
    b, s, slots = qm.shape
    n_pairs = MLA_HEADS // 2
    assert s % tq == 0 and s % tk == 0 and slots == MLA_HEADS * HEAD_SLOT
    pair_w = 2 * HEAD_SLOT
    qblk = pl.BlockSpec((1, tq, pair_w), lambda bi, p, t: (bi, t, p))
    keys = pl.BlockSpec((1, s, pair_w), lambda bi, p, t: (bi, 0, p))
    return pl.pallas_call(
        functools.partial(_mla_kernel, tk=tk),
        grid=(b, n_pairs, s // tq),
        in_specs=[qblk, keys, keys],
        out_specs=pl.BlockSpec((1, tq, 2 * MLA_V_DIM), lambda bi, p, t: (bi, t, p)),
        out_shape=jax.ShapeDtypeStruct((b, s, MLA_WIDTH), jnp.bfloat16),
        scratch_shapes=[pltpu.VMEM((2, s // tk, tq, tk), jnp.float32),
                        pltpu.VMEM((2, tq, HEAD_SLOT), jnp.float32)],
        compiler_params=pltpu.CompilerParams(
            dimension_semantics=("arbitrary", "arbitrary", "arbitrary"),
            vmem_limit_bytes=VMEM_LIMIT),
        name="mla_attn",
    )(qm, kr, kv)


def _rope_tables(seq):
    t = jnp.arange(seq)
    row = (t // GRID_W).astype(jnp.float32)
    col = (t % GRID_W).astype(jnp.float32)
    n_freq = MLA_QK_ROPE // 4
    inv_freq = 1.0 / (ROPE_THETA ** (jnp.arange(n_freq, dtype=jnp.float32) / n_freq))
    ang = jnp.concatenate([row[:, None] * inv_freq[None, :], col[:, None] * inv_freq[None, :]], axis=-1)
    cos, sin = jnp.cos(ang), jnp.sin(ang)
    zeros = lambda width: jnp.zeros((seq, width), jnp.float32)
    tail = LANES - ROT_LANE0 - MLA_QK_ROPE
    cos_t = jnp.concatenate([jnp.ones((seq, ROT_LANE0), jnp.float32), cos, cos, zeros(tail)], axis=-1)
    sin_lo = jnp.concatenate([zeros(ROT_LANE0), -sin, zeros(HALF_ROT + tail)], axis=-1)
    sin_hi = jnp.concatenate([zeros(ROT_LANE0 + HALF_ROT), sin, zeros(tail)], axis=-1)
    return cos_t, sin_lo, sin_hi


def _pad_cols(w, left, right):
    return jnp.pad(w, ((0, 0), (left, right)))


def kernel(x, ffn1_pre_g, ffn1_w_gu, ffn1_w_down, ffn1_post_g, mix_pre_g, w_in, na_rpb, mla_q_norm_g, mla_w_uq, mla_kv_norm_g, mla_w_ukv, na_out_norm_g, mla_out_norm_g, w_out, mix_post_g, ffn2_pre_g, ffn2_w_gu, ffn2_w_down, ffn2_post_g):
    B, S, D = x.shape
    depth = ffn1_w_gu.shape[0]
    d_ff = ffn1_w_down.shape[1]
    bf = jnp.bfloat16
    n_rows = S // GRID_W
    cos_t, sin_lo, sin_hi = _rope_tables(S)
    row2 = lambda g: g.reshape(1, -1)

    def ffn_params(pre_g, w_gu, w_down, post_g):
        return (row2(pre_g), w_gu[:, :d_ff].astype(bf), w_gu[:, d_ff:].astype(bf), w_down.astype(bf),
                row2(post_g))

    h = x
    for l in range(depth):
        h = _ffn(h, ffn_params(ffn1_pre_g[l], ffn1_w_gu[l], ffn1_w_down[l], ffn1_post_g[l]))

        n_main = 3 * NA_WIDTH + MLA_Q_LORA + MLA_KV_LORA
        w_in_l = jnp.concatenate(
            [w_in[l, :, :n_main],
             _pad_cols(w_in[l, :, n_main:], ROT_LANE0, LANES - ROT_LANE0 - MLA_QK_ROPE)], axis=-1).astype(bf)
        qk = MLA_QK_NOPE + MLA_QK_ROPE
        w_uq_l = jnp.pad(mla_w_uq[l].reshape(MLA_Q_LORA, MLA_HEADS, qk),
                         ((0, 0), (0, 0), (0, HEAD_SLOT - qk))).reshape(MLA_Q_LORA, MLA_HEADS * HEAD_SLOT).astype(bf)
        qna, kna, vna, qm, kr, v1 = _proj(
            h, row2(mix_pre_g[l]), w_in_l, row2(mla_q_norm_g[l]), w_uq_l, row2(mla_kv_norm_g[l]),
            mla_w_ukv[l].astype(bf), cos_t, sin_lo, sin_hi)

        bias = _na_bias(na_rpb[l].astype(jnp.float32), n_rows)
        o_na = _na_attention(qna, kna, vna, bias)
        o_mla = _mla_attention(qm, kr, v1)

        mix = (o_na, o_mla, row2(na_out_norm_g[l]), row2(mla_out_norm_g[l]),
               w_out[l, :NA_WIDTH].astype(bf), w_out[l, NA_WIDTH:].astype(bf), row2(mix_post_g[l]))
        h = _ffn(h, ffn_params(ffn2_pre_g[l], ffn2_w_gu[l], ffn2_w_down[l], ffn2_post_g[l]), mix=mix)
    return h
```

```python
import functools

import numpy as np
import jax
import jax.numpy as jnp
from jax import lax
from jax.experimental import pallas as pl
from jax.experimental.pallas import tpu as pltpu

GRID_W = 64
EPS = 1e-6
NEG_INF = -1e30

NA_HEADS = 8
NA_HEAD_DIM = 64
NA_WIN_H = 8
NA_WIN_W = 16
NA_WIDTH = NA_HEADS * NA_HEAD_DIM

MLA_HEADS = 8
MLA_QK_NOPE = 64
MLA_QK_ROPE = 32
MLA_V_DIM = 64
MLA_Q_LORA = 256
MLA_KV_LORA = 128
MLA_WIDTH = MLA_HEADS * MLA_V_DIM
ROPE_THETA = 10000.0

MXU_WIDTH = 256
LANES = 128
HEAD_SLOT = LANES
ROT_LANE0 = MLA_QK_NOPE
HALF_ROT = MLA_QK_ROPE // 2

NA_GROUP_ROWS = 4
NA_KEY_ROWS = 2 * ((NA_GROUP_ROWS + NA_WIN_H) // 2)

VMEM_LIMIT = 56 * 1024 * 1024

_NT = (((1,), (1,)), ((), ()))
LOG2_E = float(np.log2(np.e))


def _rms(x, g):
    return x * lax.rsqrt(jnp.mean(x * x, axis=-1, keepdims=True) + EPS) * g


def _bdot(a, b):
    return jnp.dot(a, b, preferred_element_type=jnp.float32)


def _token_rows(seq, tm, width):
    tiles = seq // tm
    return pl.BlockSpec((None, tm, width), lambda i: (i // tiles, i % tiles, 0))


def _resident(a):
    return pl.BlockSpec(a.shape, lambda i: (0,) * a.ndim)


def _swiglu(xn, wg_ref, wu_ref, wd_ref, ff_chunk):
    d_ff = wg_ref.shape[1]
    acc = jnp.zeros((xn.shape[0], wd_ref.shape[1]), jnp.float32)
    for c0 in range(0, d_ff, ff_chunk):
        c1 = min(c0 + ff_chunk, d_ff)
        gate = _bdot(xn, wg_ref[:, c0:c1])
        up = _bdot(xn, wu_ref[:, c0:c1])
        act = (gate * jax.nn.sigmoid(gate) * up).astype(jnp.bfloat16)
        acc = acc + _bdot(act, wd_ref[c0:c1, :])
    return acc


def _ffn_half_steps(xs, pre_g_ref, wg_ref, wu_ref, wd_ref, post_g_ref, ff_chunk):
    xns = [_rms(x, pre_g_ref[...]).astype(jnp.bfloat16) for x in xs]
    accs = [_swiglu(xn, wg_ref, wu_ref, wd_ref, ff_chunk) for xn in xns]
    return [x + 0.5 * _rms(acc, post_g_ref[...]) for x, acc in zip(xs, accs)]


def _sub_tiles(n_rows, sub_rows):
    return [slice(r0, r0 + sub_rows) for r0 in range(0, n_rows, sub_rows)]


def _ffn_kernel(x_ref, pre_g_ref, wg_ref, wu_ref, wd_ref, post_g_ref, o_ref, *, ff_chunk, sub_rows):
    tiles = _sub_tiles(x_ref.shape[0], sub_rows)
    outs = _ffn_half_steps([x_ref[rows] for rows in tiles], pre_g_ref, wg_ref, wu_ref, wd_ref,
                           post_g_ref, ff_chunk)
    for rows, out in zip(tiles, outs):
        o_ref[rows] = out


def _mix_ffn_kernel(h_ref, ona_ref, omla_ref, na_g_ref, mla_g_ref, w_na_ref, w_mla_ref, mix_post_g_ref,
                    pre_g_ref, wg_ref, wu_ref, wd_ref, post_g_ref, o_ref, *, ff_chunk, sub_rows):
    tiles = _sub_tiles(h_ref.shape[0], sub_rows)
    hs = []
    for rows in tiles:
        na = _rms(ona_ref[rows].astype(jnp.float32), na_g_ref[...]).astype(jnp.bfloat16)
        mla = _rms(omla_ref[rows].astype(jnp.float32), mla_g_ref[...]).astype(jnp.bfloat16)
        mixed = _bdot(na, w_na_ref[...]) + _bdot(mla, w_mla_ref[...])
        hs.append(h_ref[rows] + _rms(mixed, mix_post_g_ref[...]))
    outs = _ffn_half_steps(hs, pre_g_ref, wg_ref, wu_ref, wd_ref, post_g_ref, ff_chunk)
    for rows, out in zip(tiles, outs):
        o_ref[rows] = out


def _ffn(x, ffn_params, mix=None, *, tm=1024, sub_rows=256, ff_chunk=6 * MXU_WIDTH):
    b, s, d = x.shape
    d_ff = ffn_params[1].shape[1]
    assert s % tm == 0 and tm % sub_rows == 0 and ff_chunk % MXU_WIDTH == 0 and d_ff % MXU_WIDTH == 0
    row = functools.partial(_token_rows, s, tm)
    if mix is None:
        body, operands, specs = _ffn_kernel, [x], [row(d)]
    else:
        body, operands = _mix_ffn_kernel, [x, *mix]
        specs = [row(d), row(mix[0].shape[-1]), row(mix[1].shape[-1])] + [_resident(a) for a in mix[2:]]
    return pl.pallas_call(
        functools.partial(body, ff_chunk=ff_chunk, sub_rows=sub_rows),
        grid=(b * s // tm,),
        in_specs=specs + [_resident(a) for a in ffn_params],
        out_specs=row(d),
        out_shape=jax.ShapeDtypeStruct((b, s, d), jnp.float32),
        compiler_params=pltpu.CompilerParams(
            dimension_semantics=("arbitrary",), vmem_limit_bytes=VMEM_LIMIT),
        name="ffn" if mix is None else "mix_ffn",
    )(*operands, *ffn_params)


def _rope(x, cos_t, sin_lo, sin_hi):
    from_hi = pltpu.roll(x, LANES - HALF_ROT, axis=1)
    from_lo = pltpu.roll(x, HALF_ROT, axis=1)
    return x * cos_t + from_hi * sin_lo + from_lo * sin_hi


def _proj_kernel(h_ref, pre_g_ref, w_in_ref, qn_g_ref, w_uq_ref, kvn_g_ref, w_ukv_ref,
                 cos_ref, sin_lo_ref, sin_hi_ref,
                 qna_ref, kna_ref, vna_ref, qm_ref, kr_ref, v1_ref, *, na_scale, mla_scale, sub_rows):
    w = NA_WIDTH
    tiles = _sub_tiles(h_ref.shape[0], sub_rows)
    hns = [_rms(h_ref[rows], pre_g_ref[...]).astype(jnp.bfloat16) for rows in tiles]
    z_lats = [_bdot(hn, w_in_ref[:, 3 * w:]) for hn in hns]
    latents = []
    for rows, z_lat in zip(tiles, z_lats):
        c_q = z_lat[:, :MLA_Q_LORA]
        c_kv = z_lat[:, MLA_Q_LORA:MLA_Q_LORA + MLA_KV_LORA]
        k_rope = z_lat[:, MLA_Q_LORA + MLA_KV_LORA:]
        tabs = (cos_ref[rows], sin_lo_ref[rows], sin_hi_ref[rows])
        q = _bdot(_rms(c_q, qn_g_ref[...]).astype(jnp.bfloat16), w_uq_ref[...])
        kv = _bdot(_rms(c_kv, kvn_g_ref[...]).astype(jnp.bfloat16), w_ukv_ref[...])
        latents.append((q, kv, _rope(k_rope, *tabs), tabs))

    for rows, hn in zip(tiles, hns):
        qna_ref[rows] = (_bdot(hn, w_in_ref[:, 0:w]) * na_scale).astype(jnp.bfloat16)
        kna_ref[rows] = _bdot(hn, w_in_ref[:, w:2 * w]).astype(jnp.bfloat16)
        vna_ref[rows] = _bdot(hn, w_in_ref[:, 2 * w:3 * w]).astype(jnp.bfloat16)

    lane = lax.broadcasted_iota(jnp.int32, (sub_rows, HEAD_SLOT), 1)
    nope = lane < MLA_QK_NOPE
    for rows, (q, kv, k_rot, tabs) in zip(tiles, latents):
        for h in range(MLA_HEADS):
            sl = slice(h * HEAD_SLOT, (h + 1) * HEAD_SLOT)
            qm_ref[rows, sl] = (_rope(q[:, sl], *tabs) * mla_scale).astype(jnp.bfloat16)
            kr_ref[rows, sl] = jnp.where(nope, kv[:, sl], k_rot).astype(jnp.bfloat16)
            v1_ref[rows, sl] = jnp.where(nope, 1.0, kv[:, sl]).astype(jnp.bfloat16)


def _proj(h, pre_g, w_in, qn_g, w_uq, kvn_g, w_ukv, cos_t, sin_lo, sin_hi, *, tm=1024, sub_rows=256):
    b, s, d = h.shape
    assert s % tm == 0 and tm % sub_rows == 0
    tiles_per_seq = s // tm
    row = functools.partial(_token_rows, s, tm)
    full = _resident
    tab =pl.BlockSpec((tm, LANES), lambda i: (i % tiles_per_seq, 0))
    slots = MLA_HEADS * HEAD_SLOT
    bf = jnp.bfloat16
    return pl.pallas_call(
        functools.partial(_proj_kernel, na_scale=NA_HEAD_DIM ** -0.5 * LOG2_E,
                          mla_scale=(MLA_QK_NOPE + MLA_QK_ROPE) ** -0.5 * LOG2_E, sub_rows=sub_rows),
        grid=(b * s // tm,),
        in_specs=[row(d), full(pre_g), full(w_in), full(qn_g), full(w_uq), full(kvn_g),
                  full(w_ukv), tab, tab, tab],
        out_specs=[row(NA_WIDTH), row(NA_WIDTH), row(NA_WIDTH), row(slots), row(slots), row(slots)],
        out_shape=[jax.ShapeDtypeStruct((b, s, NA_WIDTH), bf)] * 3
        + [jax.ShapeDtypeStruct((b, s, slots), bf)] * 3,
        compiler_params=pltpu.CompilerParams(
            dimension_semantics=("arbitrary",), vmem_limit_bytes=VMEM_LIMIT),
        name="proj",
    )(h, pre_g, w_in, qn_g, w_uq, kvn_g, w_ukv, cos_t, sin_lo, sin_hi)


def _na_kernel(q_ref, k_ref, v_ref, bias_ref, o_ref, *, n_rows):
    g = pl.program_id(1)
    key_row0 = jnp.clip(g * NA_GROUP_ROWS - NA_WIN_H // 2, 0, n_rows - NA_KEY_ROWS)
    start = pl.multiple_of(key_row0 * GRID_W, GRID_W)
    n_keys = NA_KEY_ROWS * GRID_W
    lane = lax.broadcasted_iota(jnp.int32, (NA_GROUP_ROWS * GRID_W, LANES), 1)
    low = lane < NA_HEAD_DIM
    ones = jnp.ones((n_keys, LANES), jnp.bfloat16)
    pairs = [slice(p * LANES, (p + 1) * LANES) for p in range(NA_HEADS // 2)]
    scores = []
    for p, sl in enumerate(pairs):
        q = q_ref[0, :, sl]
        k = k_ref[0, pl.ds(start, n_keys), sl]
        for e in range(2):
            qe = jnp.where(low if e == 0 else ~low, q, jnp.zeros_like(q))
            s = lax.dot_general(qe, k, _NT, preferred_element_type=jnp.float32)
            scores.append(s + jnp.concatenate(
                [jnp.concatenate([bias_ref[0, qr, j, 2 * p + e] for j in range(NA_KEY_ROWS // 2)], axis=1)
                 for qr in range(NA_GROUP_ROWS)], axis=0))
    weights = [jnp.exp2(s - jnp.max(s, axis=-1, keepdims=True)).astype(jnp.bfloat16) for s in scores]
    for p, sl in enumerate(pairs):
        v1 = jnp.concatenate([v_ref[0, pl.ds(start, n_keys), sl], ones], axis=1)
        outs = []
        for e in range(2):
            o = _bdot(weights[2 * p + e], v1)
            outs.append(o[:, :LANES] / o[:, LANES:])
        o_ref[0, :, sl] = jnp.where(low, outs[0], outs[1]).astype(o_ref.dtype)


def _na_bias(rpb, n_rows):
    G, KR, W = NA_GROUP_ROWS, NA_KEY_ROWS, GRID_W
    n_groups = n_rows // G
    n_pairs = KR // 2
    kh = min(NA_WIN_H, n_rows)
    n_ro, n_co = 2 * NA_WIN_H - 1, 2 * NA_WIN_W - 1
    qc = np.arange(W)[:, None]
    kc = np.arange(W)[None, :]
    win0 = np.clip(qc - NA_WIN_W // 2, 0, W - NA_WIN_W)
    col_ok = (kc >= win0) & (kc < win0 + NA_WIN_W)
    col_off = np.clip(kc - qc + NA_WIN_W - 1, 0, n_co - 1)
    col_sel = np.zeros((2, n_co, W, 2, W), np.float32)
    for half in range(2):
        col_sel[half, :, :, half, :] = np.eye(n_co, dtype=np.float32)[:, col_off]
    col_sel = jnp.asarray(col_sel.reshape(2 * n_co, W, 2 * W))
    rpb_pad = jnp.pad(rpb, ((0, 0), (1, 1), (0, 0)))
    first_second = jnp.concatenate([rpb_pad[:, :-1], rpb_pad[:, 1:]], axis=-1)
    pair_tab = jnp.einsum('hik,kql->hiql', first_second, col_sel, precision=lax.Precision.HIGHEST)
    col_ok2 = np.tile(col_ok, (1, 2))
    pair_tab = jnp.where(col_ok2, pair_tab * LOG2_E, NEG_INF)
    half_masks = np.zeros((2, W, 2 * W), np.float32)
    half_masks[0, :, :W] = NEG_INF
    half_masks[1, :, W:] = NEG_INF
    table = jnp.concatenate(
        [pair_tab, jnp.broadcast_to(jnp.asarray(half_masks), (rpb.shape[0],) + half_masks.shape)], axis=1)

    sels = []
    for g in (0, 1, n_groups - 1):
        r = g * G + np.arange(G)[:, None]
        key_row0 = np.clip(g * G - NA_WIN_H // 2, 0, n_rows - KR)
        kr = key_row0 + np.arange(KR)[None, :]
        row0 = np.clip(r - kh // 2, 0, n_rows - kh)
        row_bad = ~((kr >= row0) & (kr < row0 + kh))
        first_off = np.clip(kr[:, 0::2] - r + NA_WIN_H - 1, -1, n_ro - 1)
        pick = np.eye(n_ro + 1, dtype=np.float32)[first_off + 1]
        flags = np.stack([row_bad[:, 0::2], row_bad[:, 1::2]], axis=-1).astype(np.float32)
        sels.append(np.concatenate([pick, flags], axis=-1))
    sel = jnp.asarray(np.stack(sels))
    return jnp.einsum('vgji,hiql->vgjhql', sel, table, precision=lax.Precision.HIGHEST)


def _na_attention(q, k, v, bias):
    b, s, w = q.shape
    n_rows = s // GRID_W
    assert n_rows % NA_GROUP_ROWS == 0 and n_rows >= NA_KEY_ROWS and n_rows // NA_GROUP_ROWS >= 3
    n_groups = n_rows // NA_GROUP_ROWS
    tq = NA_GROUP_ROWS * GRID_W

    def bias_map(bi, g):
        variant = jnp.where(g == 0, 0, jnp.where(g == n_groups - 1, 2, 1))
        return (variant, 0, 0, 0, 0, 0)

    whole = pl.BlockSpec((1, s, w), lambda bi, g: (bi, 0, 0))
    qblk = pl.BlockSpec((1, tq, w), lambda bi, g: (bi, g, 0))
    return pl.pallas_call(
        functools.partial(_na_kernel, n_rows=n_rows),
        grid=(b, n_groups),
        in_specs=[qblk, whole, whole, pl.BlockSpec((1,) + bias.shape[1:], bias_map)],
        out_specs=qblk,
        out_shape=jax.ShapeDtypeStruct((b, s, w), jnp.bfloat16),
        compiler_params=pltpu.CompilerParams(
            dimension_semantics=("arbitrary", "arbitrary"), vmem_limit_bytes=VMEM_LIMIT),
        name="na_attn",
    )(q, k, v, bias)


def _mla_kernel(q_ref, kr_ref, kv_ref, o_ref, s_ref, acc_ref, *, tk):
    n_chunks = kr_ref.shape[1] // tk
    tq = q_ref.shape[1]
    slots = [slice(e * HEAD_SLOT, (e + 1) * HEAD_SLOT) for e in range(2)]

    def score_chunk(c, m):
        keys = pl.ds(pl.multiple_of(c * tk, tk), tk)
        new_m = []
        for e, sl in enumerate(slots):
            s = lax.dot_general(q_ref[0, :, sl], kr_ref[0, keys, sl], _NT,
                                preferred_element_type=jnp.float32)
            s_ref[e, c] = s
            new_m.append(jnp.maximum(m[e], jnp.max(s, axis=-1, keepdims=True)))
        return tuple(new_m)

    m = lax.fori_loop(0, n_chunks, score_chunk,
                      tuple(jnp.full((tq, 1), -jnp.inf, jnp.float32) for _ in slots))

    acc_ref[...] = jnp.zeros(acc_ref.shape, jnp.float32)

    @pl.loop(0, n_chunks)
    def _(c):
        keys = pl.ds(pl.multiple_of(c * tk, tk), tk)
        for e, sl in enumerate(slots):
            p = jnp.exp2(s_ref[e, c] - m[e])
            acc_ref[e] += _bdot(p.astype(jnp.bfloat16), kv_ref[0, keys, sl])

    lane = lax.broadcasted_iota(jnp.int32, (tq, HEAD_SLOT), 1)
    acc = [acc_ref[e] for e in range(2)]
    swapped = [pltpu.roll(a, LANES - MLA_V_DIM, axis=1) for a in acc]
    o_ref[0] = jnp.where(lane < MLA_V_DIM, swapped[0] / acc[0], acc[1] / swapped[1]).astype(o_ref.dtype)


def _mla_attention(qm, kr, kv, *, tq=1024, tk=4096):
    b, s, slots = qm.shape
    n_pairs = MLA_HEADS // 2
    assert s % tq == 0 and s % tk == 0 and slots == MLA_HEADS * HEAD_SLOT
    pair_w = 2 * HEAD_SLOT
    qblk = pl.BlockSpec((1, tq, pair_w), lambda bi, p, t: (bi, t, p))
    keys = pl.BlockSpec((1, s, pair_w), lambda bi, p, t: (bi, 0, p))
    return pl.pallas_call(
        functools.partial(_mla_kernel, tk=tk),
        grid=(b, n_pairs, s // tq),
        in_specs=[qblk, keys, keys],
        out_specs=pl.BlockSpec((1, tq, 2 * MLA_V_DIM), lambda bi, p, t: (bi, t, p)),
        out_shape=jax.ShapeDtypeStruct((b, s, MLA_WIDTH), jnp.bfloat16),
        scratch_shapes=[pltpu.VMEM((2, s // tk, tq, tk), jnp.float32),
                        pltpu.VMEM((2, tq, HEAD_SLOT), jnp.float32)],
        compiler_params=pltpu.CompilerParams(
            dimension_semantics=("arbitrary", "arbitrary", "arbitrary"),
            vmem_limit_bytes=VMEM_LIMIT),
        name="mla_attn",
    )(qm, kr, kv)


def _rope_tables(seq):
    t = jnp.arange(seq)
    row = (t // GRID_W).astype(jnp.float32)
    col = (t % GRID_W).astype(jnp.float32)
    n_freq = MLA_QK_ROPE // 4
    inv_freq = 1.0 / (ROPE_THETA ** (jnp.arange(n_freq, dtype=jnp.float32) / n_freq))
    ang = jnp.concatenate([row[:, None] * inv_freq[None, :], col[:, None] * inv_freq[None, :]], axis=-1)
    cos, sin = jnp.cos(ang), jnp.sin(ang)
    zeros = lambda width: jnp.zeros((seq, width), jnp.float32)
    tail = LANES - ROT_LANE0 - MLA_QK_ROPE
    cos_t = jnp.concatenate([jnp.ones((seq, ROT_LANE0), jnp.float32), cos, cos, zeros(tail)], axis=-1)
    sin_lo = jnp.concatenate([zeros(ROT_LANE0), -sin, zeros(HALF_ROT + tail)], axis=-1)
    sin_hi = jnp.concatenate([zeros(ROT_LANE0 + HALF_ROT), sin, zeros(tail)], axis=-1)
    return cos_t, sin_lo, sin_hi


def _pad_cols(w, left, right):
    return jnp.pad(w, ((0, 0), (left, right)))


def kernel(x, ffn1_pre_g, ffn1_w_gu, ffn1_w_down, ffn1_post_g, mix_pre_g, w_in, na_rpb, mla_q_norm_g, mla_w_uq, mla_kv_norm_g, mla_w_ukv, na_out_norm_g, mla_out_norm_g, w_out, mix_post_g, ffn2_pre_g, ffn2_w_gu, ffn2_w_down, ffn2_post_g):
    B, S, D = x.shape
    depth = ffn1_w_gu.shape[0]
    d_ff = ffn1_w_down.shape[1]
    bf = jnp.bfloat16
    n_rows = S // GRID_W
    cos_t, sin_lo, sin_hi = _rope_tables(S)
    row2 = lambda g: g.reshape(1, -1)

    def ffn_params(pre_g, w_gu, w_down, post_g):
        return (row2(pre_g), w_gu[:, :d_ff].astype(bf), w_gu[:, d_ff:].astype(bf), w_down.astype(bf),
                row2(post_g))

    h = x
    for l in range(depth):
        h = _ffn(h, ffn_params(ffn1_pre_g[l], ffn1_w_gu[l], ffn1_w_down[l], ffn1_post_g[l]))

        n_main = 3 * NA_WIDTH + MLA_Q_LORA + MLA_KV_LORA
        w_in_l = jnp.concatenate(
            [w_in[l, :, :n_main],
             _pad_cols(w_in[l, :, n_main:], ROT_LANE0, LANES - ROT_LANE0 - MLA_QK_ROPE)], axis=-1).astype(bf)
        qk = MLA_QK_NOPE + MLA_QK_ROPE
        w_uq_l = jnp.pad(mla_w_uq[l].reshape(MLA_Q_LORA, MLA_HEADS, qk),
                         ((0, 0), (0, 0), (0, HEAD_SLOT - qk))).reshape(MLA_Q_LORA, MLA_HEADS * HEAD_SLOT).astype(bf)
        qna, kna, vna, qm, kr, v1 = _proj(
            h, row2(mix_pre_g[l]), w_in_l, row2(mla_q_norm_g[l]), w_uq_l, row2(mla_kv_norm_g[l]),
            mla_w_ukv[l].astype(bf), cos_t, sin_lo, sin_hi)

        bias = _na_bias(na_rpb[l].astype(jnp.float32), n_rows)
        o_na = _na_attention(qna, kna, vna, bias)
        o_mla = _mla_attention(qm, kr, v1)

        mix = (o_na, o_mla, row2(na_out_norm_g[l]), row2(mla_out_norm_g[l]),
               w_out[l, :NA_WIDTH].astype(bf), w_out[l, NA_WIDTH:].astype(bf), row2(mix_post_g[l]))
        h = _ffn(h, ffn_params(ffn2_pre_g[l], ffn2_w_gu[l], ffn2_w_down[l], ffn2_post_g[l]), mix=mix)
    return h
```

```python
import functools

import numpy as np
import jax
import jax.numpy as jnp
from jax import lax
from jax.experimental import pallas as pl
from jax.experimental.pallas import tpu as pltpu

GRID_W = 64
EPS = 1e-6
NEG_INF = -1e30

NA_HEADS = 8
NA_HEAD_DIM = 64
NA_WIN_H = 8
NA_WIN_W = 16
NA_WIDTH = NA_HEADS * NA_HEAD_DIM

MLA_HEADS = 8
MLA_QK_NOPE = 64
MLA_QK_ROPE = 32
MLA_V_DIM = 64
MLA_Q_LORA = 256
MLA_KV_LORA = 128
MLA_WIDTH = MLA_HEADS * MLA_V_DIM
ROPE_THETA = 10000.0

MXU_WIDTH = 256
LANES = 128
HEAD_SLOT = LANES
ROT_LANE0 = MLA_QK_NOPE
HALF_ROT = MLA_QK_ROPE // 2

NA_GROUP_ROWS = 4
NA_KEY_ROWS = 2 * ((NA_GROUP_ROWS + NA_WIN_H) // 2)

VMEM_LIMIT = 56 * 1024 * 1024

_NT = (((1,), (1,)), ((), ()))
LOG2_E = float(np.log2(np.e))


def _rms(x, g):
    return x * lax.rsqrt(jnp.mean(x * x, axis=-1, keepdims=True) + EPS) * g


def _bdot(a, b):
    return jnp.dot(a, b, preferred_element_type=jnp.float32)


def _token_rows(seq, tm, width):
    tiles = seq // tm
    return pl.BlockSpec((None, tm, width), lambda i: (i // tiles, i % tiles, 0))


def _resident(a):
    return pl.BlockSpec(a.shape, lambda i: (0,) * a.ndim)


def _sub_tiles(n_rows, sub_rows):
    return [slice(r0, r0 + sub_rows) for r0 in range(0, n_rows, sub_rows)]


def _swiglu(xn, wg_ref, wu_ref, wd_ref, ff_chunk):
    d_ff = wg_ref.shape[1]
    acc = jnp.zeros((xn.shape[0], wd_ref.shape[1]), jnp.float32)
    for c0 in range(0, d_ff, ff_chunk):
        c1 = min(c0 + ff_chunk, d_ff)
        gate = _bdot(xn, wg_ref[:, c0:c1])
        up = _bdot(xn, wu_ref[:, c0:c1])
        act = (gate * jax.nn.sigmoid(gate) * up).astype(jnp.bfloat16)
        acc = acc + _bdot(act, wd_ref[c0:c1, :])
    return acc


def _ffn_half_steps(xs, pre_g_ref, wg_ref, wu_ref, wd_ref, post_g_ref, ff_chunk):
    xns = [_rms(x, pre_g_ref[...]).astype(jnp.bfloat16) for x in xs]
    accs = [_swiglu(xn, wg_ref, wu_ref, wd_ref, ff_chunk) for xn in xns]
    return [x + 0.5 * _rms(acc, post_g_ref[...]) for x, acc in zip(xs, accs)]


def _ffn_kernel(x_ref, pre_g_ref, wg_ref, wu_ref, wd_ref, post_g_ref, o_ref, *, ff_chunk, sub_rows):
    tiles = _sub_tiles(x_ref.shape[0], sub_rows)
    outs = _ffn_half_steps([x_ref[rows] for rows in tiles], pre_g_ref, wg_ref, wu_ref, wd_ref,
                           post_g_ref, ff_chunk)
    for rows, out in zip(tiles, outs):
        o_ref[rows] = out


def _mix_ffn_kernel(h_ref, ona_ref, omla_ref, na_g_ref, mla_g_ref, w_na_ref, w_mla_ref, mix_post_g_ref,
                    pre_g_ref, wg_ref, wu_ref, wd_ref, post_g_ref, o_ref, *, ff_chunk, sub_rows):
    tiles = _sub_tiles(h_ref.shape[0], sub_rows)
    hs = []
    for rows in tiles:
        na = _rms(ona_ref[rows].astype(jnp.float32), na_g_ref[...]).astype(jnp.bfloat16)
        mla = _rms(omla_ref[rows].astype(jnp.float32), mla_g_ref[...]).astype(jnp.bfloat16)
        mixed = _bdot(na, w_na_ref[...]) + _bdot(mla, w_mla_ref[...])
        hs.append(h_ref[rows] + _rms(mixed, mix_post_g_ref[...]))
    outs = _ffn_half_steps(hs, pre_g_ref, wg_ref, wu_ref, wd_ref, post_g_ref, ff_chunk)
    for rows, out in zip(tiles, outs):
        o_ref[rows] = out


def _ffn(x, ffn_params, mix=None, *, tm=1024, sub_rows=256, ff_chunk=6 * MXU_WIDTH):
    b, s, d = x.shape
    d_ff = ffn_params[1].shape[1]
    assert s % tm == 0 and tm % sub_rows == 0 and ff_chunk % MXU_WIDTH == 0 and d_ff % MXU_WIDTH == 0
    row = functools.partial(_token_rows, s, tm)
    if mix is None:
        body, operands, specs = _ffn_kernel, [x], [row(d)]
    else:
        body, operands = _mix_ffn_kernel, [x, *mix]
        specs = [row(d), row(mix[0].shape[-1]), row(mix[1].shape[-1])] + [_resident(a) for a in mix[2:]]
    return pl.pallas_call(
        functools.partial(body, ff_chunk=ff_chunk, sub_rows=sub_rows),
        grid=(b * s // tm,),
        in_specs=specs + [_resident(a) for a in ffn_params],
        out_specs=row(d),
        out_shape=jax.ShapeDtypeStruct((b, s, d), jnp.float32),
        compiler_params=pltpu.CompilerParams(
            dimension_semantics=("arbitrary",), vmem_limit_bytes=VMEM_LIMIT),
        name="ffn" if mix is None else "mix_ffn",
    )(*operands, *ffn_params)


def _rope(x, cos_t, sin_lo, sin_hi):
    from_hi = pltpu.roll(x, LANES - HALF_ROT, axis=1)
    from_lo = pltpu.roll(x, HALF_ROT, axis=1)
    return x * cos_t + from_hi * sin_lo + from_lo * sin_hi


def _proj_kernel(h_ref, pre_g_ref, w_in_ref, qn_g_ref, w_uq_ref, kvn_g_ref, w_ukv_ref,
                 cos_ref, sin_lo_ref, sin_hi_ref,
                 qna_ref, kna_ref, vna_ref, qm_ref, kr_ref, v1_ref, *, na_scale, mla_scale, sub_rows):
    w = NA_WIDTH
    tiles = _sub_tiles(h_ref.shape[0], sub_rows)
    hns = [_rms(h_ref[rows], pre_g_ref[...]).astype(jnp.bfloat16) for rows in tiles]
    z_lats = [_bdot(hn, w_in_ref[:, 3 * w:]) for hn in hns]
    latents = []
    for rows, z_lat in zip(tiles, z_lats):
        c_q = z_lat[:, :MLA_Q_LORA]
        c_kv = z_lat[:, MLA_Q_LORA:MLA_Q_LORA + MLA_KV_LORA]
        k_rope = z_lat[:, MLA_Q_LORA + MLA_KV_LORA:]
        tabs = (cos_ref[rows], sin_lo_ref[rows], sin_hi_ref[rows])
        q = _bdot(_rms(c_q, qn_g_ref[...]).astype(jnp.bfloat16), w_uq_ref[...])
        kv = _bdot(_rms(c_kv, kvn_g_ref[...]).astype(jnp.bfloat16), w_ukv_ref[...])
        latents.append((q, kv, _rope(k_rope, *tabs), tabs))

    for rows, hn in zip(tiles, hns):
        qna_ref[rows] = (_bdot(hn, w_in_ref[:, 0:w]) * na_scale).astype(jnp.bfloat16)
        kna_ref[rows] = _bdot(hn, w_in_ref[:, w:2 * w]).astype(jnp.bfloat16)
        vna_ref[rows] = _bdot(hn, w_in_ref[:, 2 * w:3 * w]).astype(jnp.bfloat16)

    lane = lax.broadcasted_iota(jnp.int32, (sub_rows, HEAD_SLOT), 1)
    nope = lane < MLA_QK_NOPE
    for rows, (q, kv, k_rot, tabs) in zip(tiles, latents):
        for h in range(MLA_HEADS):
            sl = slice(h * HEAD_SLOT, (h + 1) * HEAD_SLOT)
            qm_ref[rows, sl] = (_rope(q[:, sl], *tabs) * mla_scale).astype(jnp.bfloat16)
            kr_ref[rows, sl] = jnp.where(nope, kv[:, sl], k_rot).astype(jnp.bfloat16)
            v1_ref[rows, sl] = jnp.where(nope, 1.0, kv[:, sl]).astype(jnp.bfloat16)


def _proj(h, pre_g, w_in, qn_g, w_uq, kvn_g, w_ukv, cos_t, sin_lo, sin_hi, *, tm=1024, sub_rows=256):
    b, s, d = h.shape
    assert s % tm == 0 and tm % sub_rows == 0
    tiles_per_seq = s // tm
    row = functools.partial(_token_rows, s, tm)
    full = _resident
    tab = pl.BlockSpec((tm, LANES), lambda i: (i % tiles_per_seq, 0))
    slots = MLA_HEADS * HEAD_SLOT
    bf = jnp.bfloat16
    return pl.pallas_call(
        functools.partial(_proj_kernel, na_scale=NA_HEAD_DIM ** -0.5 * LOG2_E,
                          mla_scale=(MLA_QK_NOPE + MLA_QK_ROPE) ** -0.5 * LOG2_E, sub_rows=sub_rows),
        grid=(b * s // tm,),
        in_specs=[row(d), full(pre_g), full(w_in), full(qn_g), full(w_uq), full(kvn_g),
                  full(w_ukv), tab, tab, tab],
        out_specs=[row(NA_WIDTH), row(NA_WIDTH), row(NA_WIDTH), row(slots), row(slots), row(slots)],
        out_shape=[jax.ShapeDtypeStruct((b, s, NA_WIDTH), bf)] * 3
        + [jax.ShapeDtypeStruct((b, s, slots), bf)] * 3,
        compiler_params=pltpu.CompilerParams(
            dimension_semantics=("arbitrary",), vmem_limit_bytes=VMEM_LIMIT),
        name="proj",
    )(h, pre_g, w_in, qn_g, w_uq, kvn_g, w_ukv, cos_t, sin_lo, sin_hi)


def _na_kernel(q_ref, k_ref, v_ref, bias_ref, o_ref, *, n_rows):
    g = pl.program_id(1)
    key_row0 = jnp.clip(g * NA_GROUP_ROWS - NA_WIN_H // 2, 0, n_rows - NA_KEY_ROWS)
    start = pl.multiple_of(key_row0 * GRID_W, GRID_W)
    n_keys = NA_KEY_ROWS * GRID_W
    lane = lax.broadcasted_iota(jnp.int32, (NA_GROUP_ROWS * GRID_W, LANES), 1)
    low = lane < NA_HEAD_DIM
    ones = jnp.ones((n_keys, LANES), jnp.bfloat16)
    pairs = [slice(p * LANES, (p + 1) * LANES) for p in range(NA_HEADS // 2)]
    scores = []
    for p, sl in enumerate(pairs):
        q = q_ref[0, :, sl]
        k = k_ref[0, pl.ds(start, n_keys), sl]
        for e in range(2):
            qe = jnp.where(low if e == 0 else ~low, q, jnp.zeros_like(q))
            s = lax.dot_general(qe, k, _NT, preferred_element_type=jnp.float32)
            scores.append(s + jnp.concatenate(
                [bias_ref[0, 2 * p + e, j] for j in range(NA_KEY_ROWS // 2)], axis=1))
    weights = [jnp.exp2(s - jnp.max(s, axis=-1, keepdims=True)).astype(jnp.bfloat16) for s in scores]
    for p, sl in enumerate(pairs):
        v1 = jnp.concatenate([v_ref[0, pl.ds(start, n_keys), sl], ones], axis=1)
        outs = []
        for e in range(2):
            o = _bdot(weights[2 * p + e], v1)
            outs.append(o[:, :LANES] / o[:, LANES:])
        o_ref[0, :, sl] = jnp.where(low, outs[0], outs[1]).astype(o_ref.dtype)


def _na_bias(rpb, n_rows):
    G, KR, W = NA_GROUP_ROWS, NA_KEY_ROWS, GRID_W
    n_groups = n_rows // G
    n_pairs = KR // 2
    kh = min(NA_WIN_H, n_rows)
    n_ro, n_co = 2 * NA_WIN_H - 1, 2 * NA_WIN_W - 1
    qc = np.arange(W)[:, None]
    kc = np.arange(W)[None, :]
    win0 = np.clip(qc - NA_WIN_W // 2, 0, W - NA_WIN_W)
    col_ok = (kc >= win0) & (kc < win0 + NA_WIN_W)
    col_off = np.clip(kc - qc + NA_WIN_W - 1, 0, n_co - 1)
    col_sel = np.zeros((2, n_co, W, 2, W), np.float32)
    for half in range(2):
        col_sel[half, :, :, half, :] = np.eye(n_co, dtype=np.float32)[:, col_off]
    col_sel = jnp.asarray(col_sel.reshape(2 * n_co, W, 2 * W))
    rpb_pad = jnp.pad(rpb, ((0, 0), (1, 1), (0, 0)))
    first_second = jnp.concatenate([rpb_pad[:, :-1], rpb_pad[:, 1:]], axis=-1)
    pair_tab = jnp.einsum('hik,kql->hiql', first_second, col_sel, precision=lax.Precision.HIGHEST)

    sels, oks = [], []
    for g in (0, 1, n_groups - 1):
        r = g * G + np.arange(G)[:, None]
        key_row0 = np.clip(g * G - NA_WIN_H // 2, 0, n_rows - KR)
        kr = key_row0 + np.arange(KR)[None, :]
        row0 = np.clip(r - kh // 2, 0, n_rows - kh)
        row_ok = (kr >= row0) & (kr < row0 + kh)
        first_off = np.clip(kr[:, 0::2] - r + NA_WIN_H - 1, -1, n_ro - 1)
        sels.append(np.eye(n_ro + 1, dtype=np.float32)[first_off + 1])
        ok = row_ok.reshape(G, 1, n_pairs, 2, 1) & col_ok.reshape(1, W, 1, 1, W)
        oks.append(ok.reshape(G, W, n_pairs, 2 * W).transpose(2, 0, 1, 3))
    sel = jnp.asarray(np.stack(sels))
    ok = np.stack(oks)[:, None]
    b = jnp.einsum('vgji,hiql->vhjgql', sel, pair_tab, precision=lax.Precision.HIGHEST)
    b = jnp.where(ok, b * LOG2_E, NEG_INF)
    return b.reshape(3, rpb.shape[0], n_pairs, G * W, 2 * W)


def _na_attention(q, k, v, bias):
    b, s, w = q.shape
    n_rows = s // GRID_W
    assert n_rows % NA_GROUP_ROWS == 0 and n_rows >= NA_KEY_ROWS and n_rows // NA_GROUP_ROWS >= 3
    n_groups = n_rows // NA_GROUP_ROWS
    tq = NA_GROUP_ROWS * GRID_W

    def bias_map(bi, g):
        variant = jnp.where(g == 0, 0, jnp.where(g == n_groups - 1, 2, 1))
        return (variant, 0, 0, 0, 0)

    whole = pl.BlockSpec((1, s, w), lambda bi, g: (bi, 0, 0))
    qblk = pl.BlockSpec((1, tq, w), lambda bi, g: (bi, g, 0))
    return pl.pallas_call(
        functools.partial(_na_kernel, n_rows=n_rows),
        grid=(b, n_groups),
        in_specs=[qblk, whole, whole, pl.BlockSpec((1,) + bias.shape[1:], bias_map)],
        out_specs=qblk,
        out_shape=jax.ShapeDtypeStruct((b, s, w), jnp.bfloat16),
        compiler_params=pltpu.CompilerParams(
            dimension_semantics=("arbitrary", "arbitrary"), vmem_limit_bytes=VMEM_LIMIT),
        name="na_attn",
    )(q, k, v, bias)


def _mla_kernel(q_ref, kr_ref, kv_ref, o_ref, s_ref):
    tq = q_ref.shape[1]
    slots = [slice(e * HEAD_SLOT, (e + 1) * HEAD_SLOT) for e in range(2)]
    m = []
    for e, sl in enumerate(slots):
        s = lax.dot_general(q_ref[0, :, sl], kr_ref[0, :, sl], _NT, preferred_element_type=jnp.float32)
        s_ref[e] = s
        m.append(jnp.max(s, axis=-1, keepdims=True))
    acc = []
    for e, sl in enumerate(slots):
        p = jnp.exp2(s_ref[e] - m[e])
        acc.append(_bdot(p.astype(jnp.bfloat16), kv_ref[0, :, sl]))

    lane = lax.broadcasted_iota(jnp.int32, (tq, HEAD_SLOT), 1)
    swapped = [pltpu.roll(a, LANES - MLA_V_DIM, axis=1) for a in acc]
    o_ref[0] = jnp.where(lane < MLA_V_DIM, swapped[0] / acc[0], acc[1] / swapped[1]).astype(o_ref.dtype)


def _mla_attention(qm, kr, kv, *, tq=1024):
    b, s, slots = qm.shape
    n_pairs = MLA_HEADS // 2
    assert s % tq == 0 and slots == MLA_HEADS * HEAD_SLOT
    pair_w = 2 * HEAD_SLOT
    qblk = pl.BlockSpec((1, tq, pair_w), lambda bi, p, t: (bi, t, p))
    keys = pl.BlockSpec((1, s, pair_w), lambda bi, p, t: (bi, 0, p))
    return pl.pallas_call(
        _mla_kernel,
        grid=(b, n_pairs, s // tq),
        in_specs=[qblk, keys, keys],
        out_specs=pl.BlockSpec((1, tq, 2 * MLA_V_DIM), lambda bi, p, t: (bi, t, p)),
        out_shape=jax.ShapeDtypeStruct((b, s, MLA_WIDTH), jnp.bfloat16),
        scratch_shapes=[pltpu.VMEM((2, tq, s), jnp.float32)],
        compiler_params=pltpu.CompilerParams(
            dimension_semantics=("arbitrary", "arbitrary", "arbitrary"),
            vmem_limit_bytes=VMEM_LIMIT),
        name="mla_attn",
    )(qm, kr, kv)


def _rope_tables(seq):
    t = jnp.arange(seq)
    row = (t // GRID_W).astype(jnp.float32)
    col = (t % GRID_W).astype(jnp.float32)
    n_freq = MLA_QK_ROPE // 4
    inv_freq = 1.0 / (ROPE_THETA ** (jnp.arange(n_freq, dtype=jnp.float32) / n_freq))
    ang = jnp.concatenate([row[:, None] * inv_freq[None, :], col[:, None] * inv_freq[None, :]], axis=-1)
    cos, sin = jnp.cos(ang), jnp.sin(ang)
    zeros = lambda width: jnp.zeros((seq, width), jnp.float32)
    tail = LANES - ROT_LANE0 - MLA_QK_ROPE
    cos_t = jnp.concatenate([jnp.ones((seq, ROT_LANE0), jnp.float32), cos, cos, zeros(tail)], axis=-1)
    sin_lo = jnp.concatenate([zeros(ROT_LANE0), -sin, zeros(HALF_ROT + tail)], axis=-1)
    sin_hi = jnp.concatenate([zeros(ROT_LANE0 + HALF_ROT), sin, zeros(tail)], axis=-1)
    return cos_t, sin_lo, sin_hi


def _pad_cols(w, left, right):
    return jnp.pad(w, ((0, 0), (left, right)))


def kernel(x, ffn1_pre_g, ffn1_w_gu, ffn1_w_down, ffn1_post_g, mix_pre_g, w_in, na_rpb, mla_q_norm_g, mla_w_uq, mla_kv_norm_g, mla_w_ukv, na_out_norm_g, mla_out_norm_g, w_out, mix_post_g, ffn2_pre_g, ffn2_w_gu, ffn2_w_down, ffn2_post_g):
    B, S, D = x.shape
    depth = ffn1_w_gu.shape[0]
    d_ff = ffn1_w_down.shape[1]
    bf = jnp.bfloat16
    n_rows = S // GRID_W
    cos_t, sin_lo, sin_hi = _rope_tables(S)
    row2 = lambda g: g.reshape(1, -1)

    def ffn_params(pre_g, w_gu, w_down, post_g):
        return (row2(pre_g), w_gu[:, :d_ff].astype(bf), w_gu[:, d_ff:].astype(bf), w_down.astype(bf),
                row2(post_g))

    h = x
    for l in range(depth):
        h = _ffn(h, ffn_params(ffn1_pre_g[l], ffn1_w_gu[l], ffn1_w_down[l], ffn1_post_g[l]))

        n_main = 3 * NA_WIDTH + MLA_Q_LORA + MLA_KV_LORA
        w_in_l = jnp.concatenate(
            [w_in[l, :, :n_main],
             _pad_cols(w_in[l, :, n_main:], ROT_LANE0, LANES - ROT_LANE0 - MLA_QK_ROPE)], axis=-1).astype(bf)
        qk = MLA_QK_NOPE + MLA_QK_ROPE
        w_uq_l = jnp.pad(mla_w_uq[l].reshape(MLA_Q_LORA, MLA_HEADS, qk),
                         ((0, 0), (0, 0), (0, HEAD_SLOT - qk))).reshape(MLA_Q_LORA, MLA_HEADS * HEAD_SLOT).astype(bf)
        qna, kna, vna, qm, kr, v1 = _proj(
            h, row2(mix_pre_g[l]), w_in_l, row2(mla_q_norm_g[l]), w_uq_l, row2(mla_kv_norm_g[l]),
            mla_w_ukv[l].astype(bf), cos_t, sin_lo, sin_hi)

        bias = _na_bias(na_rpb[l].astype(jnp.float32), n_rows)
        o_na = _na_attention(qna, kna, vna, bias)
        o_mla = _mla_attention(qm, kr, v1)

        mix = (o_na, o_mla, row2(na_out_norm_g[l]), row2(mla_out_norm_g[l]),
               w_out[l, :NA_WIDTH].astype(bf), w_out[l, NA_WIDTH:].astype(bf), row2(mix_post_g[l]))
        h = _ffn(h, ffn_params(ffn2_pre_g[l], ffn2_w_gu[l], ffn2_w_down[l], ffn2_post_g[l]), mix=mix)
    return h
```

```python
import functools

import numpy as np
import jax
import jax.numpy as jnp
from jax import lax
from jax.experimental import pallas as pl
from jax.experimental.pallas import tpu as pltpu

GRID_W = 64
EPS = 1e-6
NEG_INF = -1e30

NA_HEADS = 8
NA_HEAD_DIM = 64
NA_WIN_H = 8
NA_WIN_W = 16
NA_WIDTH = NA_HEADS * NA_HEAD_DIM

MLA_HEADS = 8
MLA_QK_NOPE = 64
MLA_QK_ROPE = 32
MLA_V_DIM = 64
MLA_Q_LORA = 256
MLA_KV_LORA = 128
MLA_WIDTH = MLA_HEADS * MLA_V_DIM
ROPE_THETA = 10000.0

MXU_WIDTH = 256
LANES = 128
HEAD_SLOT = LANES
ROT_LANE0 = MLA_QK_NOPE
HALF_ROT = MLA_QK_ROPE // 2

NA_GROUP_ROWS = 4
NA_KEY_ROWS = 2 * ((NA_GROUP_ROWS + NA_WIN_H) // 2)

VMEM_LIMIT = 56 * 1024 * 1024

_NT = (((1,), (1,)), ((), ()))
LOG2_E = float(np.log2(np.e))


def _rms(x, g):
    return x * lax.rsqrt(jnp.mean(x * x, axis=-1, keepdims=True) + EPS) * g


def _bdot(a, b):
    return jnp.dot(a, b, preferred_element_type=jnp.float32)


def _token_rows(seq, tm, width):
    tiles = seq // tm
    return pl.BlockSpec((None, tm, width), lambda i: (i // tiles, i % tiles, 0))


def _resident(a):
    return pl.BlockSpec(a.shape, lambda i: (0,) * a.ndim)


def _sub_tiles(n_rows, sub_rows):
    return [slice(r0, r0 + sub_rows) for r0 in range(0, n_rows, sub_rows)]


def _swiglu(xn, wg_ref, wu_ref, wd_ref, ff_chunk):
    d_ff = wg_ref.shape[1]
    acc = jnp.zeros((xn.shape[0], wd_ref.shape[1]), jnp.float32)
    for c0 in range(0, d_ff, ff_chunk):
        c1 = min(c0 + ff_chunk, d_ff)
        gate = _bdot(xn, wg_ref[:, c0:c1])
        up = _bdot(xn, wu_ref[:, c0:c1])
        act = (gate * jax.nn.sigmoid(gate) * up).astype(jnp.bfloat16)
        acc = acc + _bdot(act, wd_ref[c0:c1, :])
    return acc


def _ffn_half_steps(xs, pre_g_ref, wg_ref, wu_ref, wd_ref, post_g_ref, ff_chunk):
    xns = [_rms(x, pre_g_ref[...]).astype(jnp.bfloat16) for x in xs]
    accs = [_swiglu(xn, wg_ref, wu_ref, wd_ref, ff_chunk) for xn in xns]
    return [x + 0.5 * _rms(acc, post_g_ref[...]) for x, acc in zip(xs, accs)]


def _ffn_kernel(x_ref, pre_g_ref, wg_ref, wu_ref, wd_ref, post_g_ref, o_ref, *, ff_chunk, sub_rows):
    tiles = _sub_tiles(x_ref.shape[0], sub_rows)
    outs = _ffn_half_steps([x_ref[rows] for rows in tiles], pre_g_ref, wg_ref, wu_ref, wd_ref,
                           post_g_ref, ff_chunk)
    for rows, out in zip(tiles, outs):
        o_ref[rows] = out


def _mix_ffn_kernel(h_ref, ona_ref, omla_ref, na_g_ref, mla_g_ref, w_na_ref, w_mla_ref, mix_post_g_ref,
                    pre_g_ref, wg_ref, wu_ref, wd_ref, post_g_ref, o_ref, *, ff_chunk, sub_rows):
    tiles = _sub_tiles(h_ref.shape[0], sub_rows)
    hs = []
    for rows in tiles:
        na = _rms(ona_ref[rows].astype(jnp.float32), na_g_ref[...]).astype(jnp.bfloat16)
        mla = _rms(omla_ref[rows].astype(jnp.float32), mla_g_ref[...]).astype(jnp.bfloat16)
        mixed = _bdot(na, w_na_ref[...]) + _bdot(mla, w_mla_ref[...])
        hs.append(h_ref[rows] + _rms(mixed, mix_post_g_ref[...]))
    outs = _ffn_half_steps(hs, pre_g_ref, wg_ref, wu_ref, wd_ref, post_g_ref, ff_chunk)
    for rows, out in zip(tiles, outs):
        o_ref[rows] = out


def _ffn(x, ffn_params, mix=None, *, tm=1024, sub_rows=256, ff_chunk=6 * MXU_WIDTH):
    b, s, d = x.shape
    d_ff = ffn_params[1].shape[1]
    assert s % tm == 0 and tm % sub_rows == 0 and ff_chunk % MXU_WIDTH == 0 and d_ff % MXU_WIDTH == 0
    row = functools.partial(_token_rows, s, tm)
    if mix is None:
        body, operands, specs = _ffn_kernel, [x], [row(d)]
    else:
        body, operands = _mix_ffn_kernel, [x, *mix]
        specs = [row(d), row(mix[0].shape[-1]), row(mix[1].shape[-1])] + [_resident(a) for a in mix[2:]]
    return pl.pallas_call(
        functools.partial(body, ff_chunk=ff_chunk, sub_rows=sub_rows),
        grid=(b * s // tm,),
        in_specs=specs + [_resident(a) for a in ffn_params],
        out_specs=row(d),
        out_shape=jax.ShapeDtypeStruct((b, s, d), jnp.float32),
        compiler_params=pltpu.CompilerParams(
            dimension_semantics=("arbitrary",), vmem_limit_bytes=VMEM_LIMIT),
        name="ffn" if mix is None else "mix_ffn",
    )(*operands, *ffn_params)


def _rope(x, cos_t, sin_lo, sin_hi):
    from_hi = pltpu.roll(x, LANES - HALF_ROT, axis=1)
    from_lo = pltpu.roll(x, HALF_ROT, axis=1)
    return x * cos_t + from_hi * sin_lo + from_lo * sin_hi


def _proj_kernel(h_ref, pre_g_ref, w_in_ref, qn_g_ref, w_uq_ref, kvn_g_ref, w_ukv_ref,
                 cos_ref, sin_lo_ref, sin_hi_ref,
                 qna_ref, kna_ref, vna_ref, qm_ref, kr_ref, v1_ref, *, na_scale, mla_scale, sub_rows):
    w = NA_WIDTH
    tiles = _sub_tiles(h_ref.shape[0], sub_rows)
    hns = [_rms(h_ref[rows], pre_g_ref[...]).astype(jnp.bfloat16) for rows in tiles]
    z_lats = [_bdot(hn, w_in_ref[:, 3 * w:]) for hn in hns]
    latents = []
    for rows, z_lat in zip(tiles, z_lats):
        c_q = z_lat[:, :MLA_Q_LORA]
        c_kv = z_lat[:, MLA_Q_LORA:MLA_Q_LORA + MLA_KV_LORA]
        k_rope = z_lat[:, MLA_Q_LORA + MLA_KV_LORA:]
        tabs = (cos_ref[rows], sin_lo_ref[rows], sin_hi_ref[rows])
        q = _bdot(_rms(c_q, qn_g_ref[...]).astype(jnp.bfloat16), w_uq_ref[...])
        kv = _bdot(_rms(c_kv, kvn_g_ref[...]).astype(jnp.bfloat16), w_ukv_ref[...])
        latents.append((q, kv, _rope(k_rope, *tabs), tabs))

    for rows, hn in zip(tiles, hns):
        qna_ref[rows] = (_bdot(hn, w_in_ref[:, 0:w]) * na_scale).astype(jnp.bfloat16)
        kna_ref[rows] = _bdot(hn, w_in_ref[:, w:2 * w]).astype(jnp.bfloat16)
        vna_ref[rows] = _bdot(hn, w_in_ref[:, 2 * w:3 * w]).astype(jnp.bfloat16)

    lane = lax.broadcasted_iota(jnp.int32, (sub_rows, HEAD_SLOT), 1)
    nope = lane < MLA_QK_NOPE
    for rows, (q, kv, k_rot, tabs) in zip(tiles, latents):
        for h in range(MLA_HEADS):
            sl = slice(h * HEAD_SLOT, (h + 1) * HEAD_SLOT)
            qm_ref[rows, sl] = (_rope(q[:, sl], *tabs) * mla_scale).astype(jnp.bfloat16)
            kr_ref[rows, sl] = jnp.where(nope, kv[:, sl], k_rot).astype(jnp.bfloat16)
            v1_ref[rows, sl] = jnp.where(nope, 1.0, kv[:, sl]).astype(jnp.bfloat16)


def _proj(h, pre_g, w_in, qn_g, w_uq, kvn_g, w_ukv, cos_t, sin_lo, sin_hi, *, tm=1024, sub_rows=256):
    b, s, d = h.shape
    assert s % tm == 0 and tm % sub_rows == 0
    tiles_per_seq = s // tm
    row = functools.partial(_token_rows, s, tm)
    full = _resident
    tab = pl.BlockSpec((tm, LANES), lambda i: (i % tiles_per_seq, 0))
    slots = MLA_HEADS * HEAD_SLOT
    bf = jnp.bfloat16
    return pl.pallas_call(
        functools.partial(_proj_kernel, na_scale=NA_HEAD_DIM ** -0.5 * LOG2_E,
                          mla_scale=(MLA_QK_NOPE + MLA_QK_ROPE) ** -0.5 * LOG2_E, sub_rows=sub_rows),
        grid=(b * s // tm,),
        in_specs=[row(d), full(pre_g), full(w_in), full(qn_g), full(w_uq), full(kvn_g),
                  full(w_ukv), tab, tab, tab],
        out_specs=[row(NA_WIDTH), row(NA_WIDTH), row(NA_WIDTH), row(slots), row(slots), row(slots)],
        out_shape=[jax.ShapeDtypeStruct((b, s, NA_WIDTH), bf)] * 3
        + [jax.ShapeDtypeStruct((b, s, slots), bf)] * 3,
        compiler_params=pltpu.CompilerParams(
            dimension_semantics=("arbitrary",), vmem_limit_bytes=VMEM_LIMIT),
        name="proj",
    )(h, pre_g, w_in, qn_g, w_uq, kvn_g, w_ukv, cos_t, sin_lo, sin_hi)


def _na_kernel(q_ref, k_ref, v_ref, bias_ref, o_ref, *, n_rows):
    g = pl.program_id(1)
    key_row0 = jnp.clip(g * NA_GROUP_ROWS - NA_WIN_H // 2, 0, n_rows - NA_KEY_ROWS)
    start = pl.multiple_of(key_row0 * GRID_W, GRID_W)
    n_keys = NA_KEY_ROWS * GRID_W
    lane = lax.broadcasted_iota(jnp.int32, (NA_GROUP_ROWS * GRID_W, LANES), 1)
    low = lane < NA_HEAD_DIM
    ones = jnp.ones((n_keys, LANES), jnp.bfloat16)
    for pair in range(NA_HEADS // 2):
        sl = slice(pair * LANES, (pair + 1) * LANES)
        q = q_ref[0, :, sl]
        k = k_ref[0, pl.ds(start, n_keys), sl]
        v1 = jnp.concatenate([v_ref[0, pl.ds(start, n_keys), sl], ones], axis=1)
        outs = []
        for e in range(2):
            qe = jnp.where(low if e == 0 else ~low, q, jnp.zeros_like(q))
            s = lax.dot_general(qe, k, _NT, preferred_element_type=jnp.float32)
            s = s + jnp.concatenate([bias_ref[0, 2 * pair + e, j] for j in range(NA_KEY_ROWS // 2)], axis=1)
            p = jnp.exp2(s - jnp.max(s, axis=-1, keepdims=True))
            o = _bdot(p.astype(jnp.bfloat16), v1)
            outs.append(o[:, :LANES] / o[:, LANES:])
        o_ref[0, :, sl] = jnp.where(low, outs[0], outs[1]).astype(o_ref.dtype)


def _na_bias(rpb, n_rows):
    G, KR, W = NA_GROUP_ROWS, NA_KEY_ROWS, GRID_W
    n_groups = n_rows // G
    n_pairs = KR // 2
    kh = min(NA_WIN_H, n_rows)
    n_ro, n_co = 2 * NA_WIN_H - 1, 2 * NA_WIN_W - 1
    qc = np.arange(W)[:, None]
    kc = np.arange(W)[None, :]
    win0 = np.clip(qc - NA_WIN_W // 2, 0, W - NA_WIN_W)
    col_ok = (kc >= win0) & (kc < win0 + NA_WIN_W)
    col_off = np.clip(kc - qc + NA_WIN_W - 1, 0, n_co - 1)
    col_sel = np.zeros((2, n_co, W, 2, W), np.float32)
    for half in range(2):
        col_sel[half, :, :, half, :] = np.eye(n_co, dtype=np.float32)[:, col_off]
    col_sel = jnp.asarray(col_sel.reshape(2 * n_co, W, 2 * W))
    rpb_pad = jnp.pad(rpb, ((0, 0), (1, 1), (0, 0)))
    first_second = jnp.concatenate([rpb_pad[:, :-1], rpb_pad[:, 1:]], axis=-1)
    pair_tab = jnp.einsum('hik,kql->hiql', first_second, col_sel, precision=lax.Precision.HIGHEST)

    sels, oks = [], []
    for g in (0, 1, n_groups - 1):
        r = g * G + np.arange(G)[:, None]
        key_row0 = np.clip(g * G - NA_WIN_H // 2, 0, n_rows - KR)
        kr = key_row0 + np.arange(KR)[None, :]
        row0 = np.clip(r - kh // 2, 0, n_rows - kh)
        row_ok = (kr >= row0) & (kr < row0 + kh)
        first_off = np.clip(kr[:, 0::2] - r + NA_WIN_H - 1, -1, n_ro - 1)
        sels.append(np.eye(n_ro + 1, dtype=np.float32)[first_off + 1])
        ok = row_ok.reshape(G, 1, n_pairs, 2, 1) & col_ok.reshape(1, W, 1, 1, W)
        oks.append(ok.reshape(G, W, n_pairs, 2 * W).transpose(2, 0, 1, 3))
    sel = jnp.asarray(np.stack(sels))
    ok = np.stack(oks)[:, None]
    b = jnp.einsum('vgji,hiql->vhjgql', sel, pair_tab, precision=lax.Precision.HIGHEST)
    b = jnp.where(ok, b * LOG2_E, NEG_INF)
    return b.reshape(3, rpb.shape[0], n_pairs, G * W, 2 * W)


def _na_attention(q, k, v, bias):
    b, s, w = q.shape
    n_rows = s // GRID_W
    assert n_rows % NA_GROUP_ROWS == 0 and n_rows >= NA_KEY_ROWS and n_rows // NA_GROUP_ROWS >= 3
    n_groups = n_rows // NA_GROUP_ROWS
    tq = NA_GROUP_ROWS * GRID_W

    def bias_map(bi, g):
        variant = jnp.where(g == 0, 0, jnp.where(g == n_groups - 1, 2, 1))
        return (variant, 0, 0, 0, 0)

    whole = pl.BlockSpec((1, s, w), lambda bi, g: (bi, 0, 0))
    qblk = pl.BlockSpec((1, tq, w), lambda bi, g: (bi, g, 0))
    return pl.pallas_call(
        functools.partial(_na_kernel, n_rows=n_rows),
        grid=(b, n_groups),
        in_specs=[qblk, whole, whole, pl.BlockSpec((1,) + bias.shape[1:], bias_map)],
        out_specs=qblk,
        out_shape=jax.ShapeDtypeStruct((b, s, w), jnp.bfloat16),
        compiler_params=pltpu.CompilerParams(
            dimension_semantics=("arbitrary", "arbitrary"), vmem_limit_bytes=VMEM_LIMIT),
        name="na_attn",
    )(q, k, v, bias)


def _mla_kernel(q_ref, kr_ref, kv_ref, o_ref, s_ref):
    tq = q_ref.shape[1]
    slots = [slice(e * HEAD_SLOT, (e + 1) * HEAD_SLOT) for e in range(2)]
    m = []
    for e, sl in enumerate(slots):
        s = lax.dot_general(q_ref[0, :, sl], kr_ref[0, :, sl], _NT, preferred_element_type=jnp.float32)
        s_ref[e] = s
        m.append(jnp.max(s, axis=-1, keepdims=True))
    acc = []
    for e, sl in enumerate(slots):
        p = jnp.exp2(s_ref[e] - m[e])
        acc.append(_bdot(p.astype(jnp.bfloat16), kv_ref[0, :, sl]))

    lane = lax.broadcasted_iota(jnp.int32, (tq, HEAD_SLOT), 1)
    swapped = [pltpu.roll(a, LANES - MLA_V_DIM, axis=1) for a in acc]
    o_ref[0] = jnp.where(lane < MLA_V_DIM, swapped[0] / acc[0], acc[1] / swapped[1]).astype(o_ref.dtype)


def _mla_attention(qm, kr, kv, *, tq=1024):
    b, s, slots = qm.shape
    n_pairs = MLA_HEADS // 2
    assert s % tq == 0 and slots == MLA_HEADS * HEAD_SLOT
    pair_w = 2 * HEAD_SLOT
    qblk = pl.BlockSpec((1, tq, pair_w), lambda bi, p, t: (bi, t, p))
    keys = pl.BlockSpec((1, s, pair_w), lambda bi, p, t: (bi, 0, p))
    return pl.pallas_call(
        _mla_kernel,
        grid=(b, n_pairs, s // tq),
        in_specs=[qblk, keys, keys],
        out_specs=pl.BlockSpec((1, tq, 2 * MLA_V_DIM), lambda bi, p, t: (bi, t, p)),
        out_shape=jax.ShapeDtypeStruct((b, s, MLA_WIDTH), jnp.bfloat16),
        scratch_shapes=[pltpu.VMEM((2, tq, s), jnp.float32)],
        compiler_params=pltpu.CompilerParams(
            dimension_semantics=("arbitrary", "arbitrary", "arbitrary"),
            vmem_limit_bytes=VMEM_LIMIT),
        name="mla_attn",
    )(qm, kr, kv)


def _rope_tables(seq):
    t = jnp.arange(seq)
    row = (t // GRID_W).astype(jnp.float32)
    col = (t % GRID_W).astype(jnp.float32)
    n_freq = MLA_QK_ROPE // 4
    inv_freq = 1.0 / (ROPE_THETA ** (jnp.arange(n_freq, dtype=jnp.float32) / n_freq))
    ang = jnp.concatenate([row[:, None] * inv_freq[None, :], col[:, None] * inv_freq[None, :]], axis=-1)
    cos, sin = jnp.cos(ang), jnp.sin(ang)
    zeros = lambda width: jnp.zeros((seq, width), jnp.float32)
    tail = LANES - ROT_LANE0 - MLA_QK_ROPE
    cos_t = jnp.concatenate([jnp.ones((seq, ROT_LANE0), jnp.float32), cos, cos, zeros(tail)], axis=-1)
    sin_lo = jnp.concatenate([zeros(ROT_LANE0), -sin, zeros(HALF_ROT + tail)], axis=-1)
    sin_hi = jnp.concatenate([zeros(ROT_LANE0 + HALF_ROT), sin, zeros(tail)], axis=-1)
    return cos_t, sin_lo, sin_hi


def _pad_cols(w, left, right):
    return jnp.pad(w, ((0, 0), (left, right)))


def kernel(x, ffn1_pre_g, ffn1_w_gu, ffn1_w_down, ffn1_post_g, mix_pre_g, w_in, na_rpb, mla_q_norm_g, mla_w_uq, mla_kv_norm_g, mla_w_ukv, na_out_norm_g, mla_out_norm_g, w_out, mix_post_g, ffn2_pre_g, ffn2_w_gu, ffn2_w_down, ffn2_post_g):
    B, S, D = x.shape
    depth = ffn1_w_gu.shape[0]
    d_ff = ffn1_w_down.shape[1]
    bf = jnp.bfloat16
    n_rows = S // GRID_W
    cos_t, sin_lo, sin_hi = _rope_tables(S)
    row2 = lambda g: g.reshape(1, -1)

    def ffn_params(pre_g, w_gu, w_down, post_g):
        return (row2(pre_g), w_gu[:, :d_ff].astype(bf), w_gu[:, d_ff:].astype(bf), w_down.astype(bf),
                row2(post_g))

    h = x
    for l in range(depth):
        h = _ffn(h, ffn_params(ffn1_pre_g[l], ffn1_w_gu[l], ffn1_w_down[l], ffn1_post_g[l]))

        n_main = 3 * NA_WIDTH + MLA_Q_LORA + MLA_KV_LORA
        w_in_l = jnp.concatenate(
            [w_in[l, :, :n_main],
             _pad_cols(w_in[l, :, n_main:], ROT_LANE0, LANES - ROT_LANE0 - MLA_QK_ROPE)], axis=-1).astype(bf)
        qk = MLA_QK_NOPE + MLA_QK_ROPE
        w_uq_l = jnp.pad(mla_w_uq[l].reshape(MLA_Q_LORA, MLA_HEADS, qk),
                         ((0, 0), (0, 0), (0, HEAD_SLOT - qk))).reshape(MLA_Q_LORA, MLA_HEADS * HEAD_SLOT).astype(bf)
        qna, kna, vna, qm, kr, v1 = _proj(
            h, row2(mix_pre_g[l]), w_in_l, row2(mla_q_norm_g[l]), w_uq_l, row2(mla_kv_norm_g[l]),
            mla_w_ukv[l].astype(bf), cos_t, sin_lo, sin_hi)

        bias = _na_bias(na_rpb[l].astype(jnp.float32), n_rows)
        o_na = _na_attention(qna, kna, vna, bias)
        o_mla = _mla_attention(qm, kr, v1)

        mix = (o_na, o_mla, row2(na_out_norm_g[l]), row2(mla_out_norm_g[l]),
               w_out[l, :NA_WIDTH].astype(bf), w_out[l, NA_WIDTH:].astype(bf), row2(mix_post_g[l]))
        h = _ffn(h, ffn_params(ffn2_pre_g[l], ffn2_w_gu[l], ffn2_w_down[l], ffn2_post_g[l]), mix=mix)
    return h
```

```python
import functools

import numpy as np
import jax
import jax.numpy as jnp
from jax import lax
from jax.experimental import pallas as pl
from jax.experimental.pallas import tpu as pltpu

GRID_W = 64
EPS = 1e-6
NEG_INF = -1e30

NA_HEADS = 8
NA_HEAD_DIM = 64
NA_WIN_H = 8
NA_WIN_W = 16
NA_WIDTH = NA_HEADS * NA_HEAD_DIM

MLA_HEADS = 8
MLA_QK_NOPE = 64
MLA_QK_ROPE = 32
MLA_V_DIM = 64
MLA_Q_LORA = 256
MLA_KV_LORA = 128
MLA_WIDTH = MLA_HEADS * MLA_V_DIM
ROPE_THETA = 10000.0

MXU_WIDTH = 256
LANES = 128
HEAD_SLOT = LANES
ROT_LANE0 = MLA_QK_NOPE
HALF_ROT = MLA_QK_ROPE // 2

NA_GROUP_ROWS = 4
NA_KEY_ROWS = 2 * ((NA_GROUP_ROWS + NA_WIN_H) // 2)

VMEM_LIMIT = 56 * 1024 * 1024

_NT = (((1,), (1,)), ((), ()))
LOG2_E = float(np.log2(np.e))


def _rms(x, g):
    return x * lax.rsqrt(jnp.mean(x * x, axis=-1, keepdims=True) + EPS) * g


def _bdot(a, b):
    return jnp.dot(a, b, preferred_element_type=jnp.float32)


def _token_rows(seq, tm, width):
    tiles = seq // tm
    return pl.BlockSpec((None, tm, width), lambda i: (i // tiles, i % tiles, 0))


def _resident(a):
    return pl.BlockSpec(a.shape, lambda i: (0,) * a.ndim)


def _sub_tiles(n_rows, sub_rows):
    return [slice(r0, r0 + sub_rows) for r0 in range(0, n_rows, sub_rows)]


def _swiglu(xn, wg_ref, wu_ref, wd_ref, ff_chunk):
    d_ff = wg_ref.shape[1]
    acc = jnp.zeros((xn.shape[0], wd_ref.shape[1]), jnp.float32)
    for c0 in range(0, d_ff, ff_chunk):
        c1 = min(c0 + ff_chunk, d_ff)
        gate = _bdot(xn, wg_ref[:, c0:c1])
        up = _bdot(xn, wu_ref[:, c0:c1])
        act = (gate * jax.nn.sigmoid(gate) * up).astype(jnp.bfloat16)
        acc = acc + _bdot(act, wd_ref[c0:c1, :])
    return acc


def _ffn_half_steps(xs, pre_g_ref, wg_ref, wu_ref, wd_ref, post_g_ref, ff_chunk):
    xns = [_rms(x, pre_g_ref[...]).astype(jnp.bfloat16) for x in xs]
    accs = [_swiglu(xn, wg_ref, wu_ref, wd_ref, ff_chunk) for xn in xns]
    return [x + 0.5 * _rms(acc, post_g_ref[...]) for x, acc in zip(xs, accs)]


def _ffn_kernel(x_ref, pre_g_ref, wg_ref, wu_ref, wd_ref, post_g_ref, o_ref, *, ff_chunk, sub_rows):
    tiles = _sub_tiles(x_ref.shape[0], sub_rows)
    outs = _ffn_half_steps([x_ref[rows] for rows in tiles], pre_g_ref, wg_ref, wu_ref, wd_ref,
                           post_g_ref, ff_chunk)
    for rows, out in zip(tiles, outs):
        o_ref[rows] = out


def _mix_ffn_kernel(h_ref, ona_ref, omla_ref, na_g_ref, mla_g_ref, w_na_ref, w_mla_ref, mix_post_g_ref,
                    pre_g_ref, wg_ref, wu_ref, wd_ref, post_g_ref, o_ref, *, ff_chunk, sub_rows):
    tiles = _sub_tiles(h_ref.shape[0], sub_rows)
    hs = []
    for rows in tiles:
        na = _rms(ona_ref[rows].astype(jnp.float32), na_g_ref[...]).astype(jnp.bfloat16)
        mla = _rms(omla_ref[rows].astype(jnp.float32), mla_g_ref[...]).astype(jnp.bfloat16)
        mixed = _bdot(na, w_na_ref[...]) + _bdot(mla, w_mla_ref[...])
        hs.append(h_ref[rows] + _rms(mixed, mix_post_g_ref[...]))
    outs = _ffn_half_steps(hs, pre_g_ref, wg_ref, wu_ref, wd_ref, post_g_ref, ff_chunk)
    for rows, out in zip(tiles, outs):
        o_ref[rows] = out


def _ffn(x, ffn_params, mix=None, *, tm=1024, sub_rows=256, ff_chunk=6 * MXU_WIDTH):
    b, s, d = x.shape
    d_ff = ffn_params[1].shape[1]
    assert s % tm == 0 and tm % sub_rows == 0 and ff_chunk % MXU_WIDTH == 0 and d_ff % MXU_WIDTH == 0
    row = functools.partial(_token_rows, s, tm)
    if mix is None:
        body, operands, specs = _ffn_kernel, [x], [row(d)]
    else:
        body, operands = _mix_ffn_kernel, [x, *mix]
        specs = [row(d), row(mix[0].shape[-1]), row(mix[1].shape[-1])] + [_resident(a) for a in mix[2:]]
    return pl.pallas_call(
        functools.partial(body, ff_chunk=ff_chunk, sub_rows=sub_rows),
        grid=(b * s // tm,),
        in_specs=specs + [_resident(a) for a in ffn_params],
        out_specs=row(d),
        out_shape=jax.ShapeDtypeStruct((b, s, d), jnp.float32),
        compiler_params=pltpu.CompilerParams(
            dimension_semantics=("arbitrary",), vmem_limit_bytes=VMEM_LIMIT),
        name="ffn" if mix is None else "mix_ffn",
    )(*operands, *ffn_params)


def _rope(x, cos_t, sin_lo, sin_hi):
    from_hi = pltpu.roll(x, LANES - HALF_ROT, axis=1)
    from_lo = pltpu.roll(x, HALF_ROT, axis=1)
    return x * cos_t + from_hi * sin_lo + from_lo * sin_hi


def _proj_kernel(h_ref, pre_g_ref, w_in_ref, qn_g_ref, w_uq_ref, kvn_g_ref, w_ukv_ref,
                 cos_ref, sin_lo_ref, sin_hi_ref,
                 qna_ref, kna_ref, vna_ref, qm_ref, kr_ref, v1_ref, *, na_scale, mla_scale, sub_rows):
    w = NA_WIDTH
    tiles = _sub_tiles(h_ref.shape[0], sub_rows)
    hns = [_rms(h_ref[rows], pre_g_ref[...]).astype(jnp.bfloat16) for rows in tiles]
    z_lats = [_bdot(hn, w_in_ref[:, 3 * w:]) for hn in hns]
    latents = []
    for rows, z_lat in zip(tiles, z_lats):
        c_q = z_lat[:, :MLA_Q_LORA]
        c_kv = z_lat[:, MLA_Q_LORA:MLA_Q_LORA + MLA_KV_LORA]
        k_rope = z_lat[:, MLA_Q_LORA + MLA_KV_LORA:]
        tabs = (cos_ref[rows], sin_lo_ref[rows], sin_hi_ref[rows])
        q = _bdot(_rms(c_q, qn_g_ref[...]).astype(jnp.bfloat16), w_uq_ref[...])
        kv = _bdot(_rms(c_kv, kvn_g_ref[...]).astype(jnp.bfloat16), w_ukv_ref[...])
        latents.append((q, kv, _rope(k_rope, *tabs), tabs))

    for rows, hn in zip(tiles, hns):
        qna_ref[rows] = (_bdot(hn, w_in_ref[:, 0:w]) * na_scale).astype(jnp.bfloat16)
        kna_ref[rows] = _bdot(hn, w_in_ref[:, w:2 * w]).astype(jnp.bfloat16)
        vna_ref[rows] = _bdot(hn, w_in_ref[:, 2 * w:3 * w]).astype(jnp.bfloat16)

    lane = lax.broadcasted_iota(jnp.int32, (sub_rows, HEAD_SLOT), 1)
    nope = lane < MLA_QK_NOPE
    for rows, (q, kv, k_rot, tabs) in zip(tiles, latents):
        for h in range(MLA_HEADS):
            sl = slice(h * HEAD_SLOT, (h + 1) * HEAD_SLOT)
            qm_ref[rows, sl] = (_rope(q[:, sl], *tabs) * mla_scale).astype(jnp.bfloat16)
            kr_ref[rows, sl] = jnp.where(nope, kv[:, sl], k_rot).astype(jnp.bfloat16)
            v1_ref[rows, sl] = jnp.where(nope, 1.0, kv[:, sl]).astype(jnp.bfloat16)


def _proj(h, pre_g, w_in, qn_g, w_uq, kvn_g, w_ukv, cos_t, sin_lo, sin_hi, *, tm=1024, sub_rows=256):
    b, s, d = h.shape
    assert s % tm == 0 and tm % sub_rows == 0
    tiles_per_seq = s // tm
    row = functools.partial(_token_rows, s, tm)
    full = _resident
    tab = pl.BlockSpec((tm, LANES), lambda i: (i % tiles_per_seq, 0))
    slots = MLA_HEADS * HEAD_SLOT
    bf = jnp.bfloat16
    return pl.pallas_call(
        functools.partial(_proj_kernel, na_scale=NA_HEAD_DIM ** -0.5 * LOG2_E,
                          mla_scale=(MLA_QK_NOPE + MLA_QK_ROPE) ** -0.5 * LOG2_E, sub_rows=sub_rows),
        grid=(b * s // tm,),
        in_specs=[row(d), full(pre_g), full(w_in), full(qn_g), full(w_uq), full(kvn_g),
                  full(w_ukv), tab, tab, tab],
        out_specs=[row(NA_WIDTH), row(NA_WIDTH), row(NA_WIDTH), row(slots), row(slots), row(slots)],
        out_shape=[jax.ShapeDtypeStruct((b, s, NA_WIDTH), bf)] * 3
        + [jax.ShapeDtypeStruct((b, s, slots), bf)] * 3,
        compiler_params=pltpu.CompilerParams(
            dimension_semantics=("arbitrary",), vmem_limit_bytes=VMEM_LIMIT),
        name="proj",
    )(h, pre_g, w_in, qn_g, w_uq, kvn_g, w_ukv, cos_t, sin_lo, sin_hi)


def _na_kernel(q_ref, k_ref, v_ref, bias_ref, o_ref, *, n_rows):
    g = pl.program_id(1)
    key_row0 = jnp.clip(g * NA_GROUP_ROWS - NA_WIN_H // 2, 0, n_rows - NA_KEY_ROWS)
    start = pl.multiple_of(key_row0 * GRID_W, GRID_W)
    n_keys = NA_KEY_ROWS * GRID_W
    lane = lax.broadcasted_iota(jnp.int32, (NA_GROUP_ROWS * GRID_W, LANES), 1)
    low = lane < NA_HEAD_DIM
    ones = jnp.ones((n_keys, LANES), jnp.bfloat16)
    for pair in range(NA_HEADS // 2):
        sl = slice(pair * LANES, (pair + 1) * LANES)
        q = q_ref[0, :, sl]
        k = k_ref[0, pl.ds(start, n_keys), sl]
        v1 = jnp.concatenate([v_ref[0, pl.ds(start, n_keys), sl], ones], axis=1)
        outs = []
        for e in range(2):
            qe = jnp.where(low if e == 0 else ~low, q, jnp.zeros_like(q))
            s = lax.dot_general(qe, k, _NT, preferred_element_type=jnp.float32)
            s = s + jnp.concatenate(
                [jnp.concatenate([bias_ref[0, qr, j, 2 * pair + e] for j in range(NA_KEY_ROWS // 2)], axis=1)
                 for qr in range(NA_GROUP_ROWS)], axis=0)
            p = jnp.exp2(s - jnp.max(s, axis=-1, keepdims=True))
            o = _bdot(p.astype(jnp.bfloat16), v1)
            outs.append(o[:, :LANES] / o[:, LANES:])
        o_ref[0, :, sl] = jnp.where(low, outs[0], outs[1]).astype(o_ref.dtype)


def _na_bias(rpb, n_rows):
    G, KR, W = NA_GROUP_ROWS, NA_KEY_ROWS, GRID_W
    n_groups = n_rows // G
    n_pairs = KR // 2
    kh = min(NA_WIN_H, n_rows)
    n_ro, n_co = 2 * NA_WIN_H - 1, 2 * NA_WIN_W - 1
    qc = np.arange(W)[:, None]
    kc = np.arange(W)[None, :]
    win0 = np.clip(qc - NA_WIN_W // 2, 0, W - NA_WIN_W)
    col_ok = (kc >= win0) & (kc < win0 + NA_WIN_W)
    col_off = np.clip(kc - qc + NA_WIN_W - 1, 0, n_co - 1)
    col_sel = np.zeros((2, n_co, W, 2, W), np.float32)
    for half in range(2):
        col_sel[half, :, :, half, :] = np.eye(n_co, dtype=np.float32)[:, col_off]
    col_sel = jnp.asarray(col_sel.reshape(2 * n_co, W, 2 * W))
    rpb_pad = jnp.pad(rpb, ((0, 0), (1, 1), (0, 0)))
    first_second = jnp.concatenate([rpb_pad[:, :-1], rpb_pad[:, 1:]], axis=-1)
    pair_tab = jnp.einsum('hik,kql->hiql', first_second, col_sel, precision=lax.Precision.HIGHEST)
    col_ok2 = np.tile(col_ok, (1, 2))
    pair_tab = jnp.where(col_ok2, pair_tab * LOG2_E, NEG_INF)
    half_masks = np.zeros((2, W, 2 * W), np.float32)
    half_masks[0, :, :W] = NEG_INF
    half_masks[1, :, W:] = NEG_INF
    table = jnp.concatenate(
        [pair_tab, jnp.broadcast_to(jnp.asarray(half_masks), (rpb.shape[0],) + half_masks.shape)], axis=1)

    sels = []
    for g in (0, 1, n_groups - 1):
        r = g * G + np.arange(G)[:, None]
        key_row0 = np.clip(g * G - NA_WIN_H // 2, 0, n_rows - KR)
        kr = key_row0 + np.arange(KR)[None, :]
        row0 = np.clip(r - kh // 2, 0, n_rows - kh)
        row_bad = ~((kr >= row0) & (kr < row0 + kh))
        first_off = np.clip(kr[:, 0::2] - r + NA_WIN_H - 1, -1, n_ro - 1)
        pick = np.eye(n_ro + 1, dtype=np.float32)[first_off + 1]
        flags = np.stack([row_bad[:, 0::2], row_bad[:, 1::2]], axis=-1).astype(np.float32)
        sels.append(np.concatenate([pick, flags], axis=-1))
    sel = jnp.asarray(np.stack(sels))
    return jnp.einsum('vgji,hiql->vgjhql', sel, table, precision=lax.Precision.HIGHEST)


def _na_attention(q, k, v, bias):
    b, s, w = q.shape
    n_rows = s // GRID_W
    assert n_rows % NA_GROUP_ROWS == 0 and n_rows >= NA_KEY_ROWS and n_rows // NA_GROUP_ROWS >= 3
    n_groups = n_rows // NA_GROUP_ROWS
    tq = NA_GROUP_ROWS * GRID_W

    def bias_map(bi, g):
        variant = jnp.where(g == 0, 0, jnp.where(g == n_groups - 1, 2, 1))
        return (variant, 0, 0, 0, 0, 0)

    whole = pl.BlockSpec((1, s, w), lambda bi, g: (bi, 0, 0))
    qblk = pl.BlockSpec((1, tq, w), lambda bi, g: (bi, g, 0))
    return pl.pallas_call(
        functools.partial(_na_kernel, n_rows=n_rows),
        grid=(b, n_groups),
        in_specs=[qblk, whole, whole, pl.BlockSpec((1,) + bias.shape[1:], bias_map)],
        out_specs=qblk,
        out_shape=jax.ShapeDtypeStruct((b, s, w), jnp.bfloat16),
        compiler_params=pltpu.CompilerParams(
            dimension_semantics=("arbitrary", "arbitrary"), vmem_limit_bytes=VMEM_LIMIT),
        name="na_attn",
    )(q, k, v, bias)


def _mla_kernel(q_ref, kr_ref, kv_ref, o_ref, s_ref):
    tq = q_ref.shape[1]
    slots = [slice(e * HEAD_SLOT, (e + 1) * HEAD_SLOT) for e in range(2)]
    m = []
    for e, sl in enumerate(slots):
        s = lax.dot_general(q_ref[0, :, sl], kr_ref[0, :, sl], _NT, preferred_element_type=jnp.float32)
        s_ref[e] = s
        m.append(jnp.max(s, axis=-1, keepdims=True))
    acc = []
    for e, sl in enumerate(slots):
        p = jnp.exp2(s_ref[e] - m[e])
        acc.append(_bdot(p.astype(jnp.bfloat16), kv_ref[0, :, sl]))

    lane = lax.broadcasted_iota(jnp.int32, (tq, HEAD_SLOT), 1)
    swapped = [pltpu.roll(a, LANES - MLA_V_DIM, axis=1) for a in acc]
    o_ref[0] = jnp.where(lane < MLA_V_DIM, swapped[0] / acc[0], acc[1] / swapped[1]).astype(o_ref.dtype)


def _mla_attention(qm, kr, kv, *, tq=1024):
    b, s, slots = qm.shape
    n_pairs = MLA_HEADS // 2
    assert s % tq == 0 and slots == MLA_HEADS * HEAD_SLOT
    pair_w = 2 * HEAD_SLOT
    qblk = pl.BlockSpec((1, tq, pair_w), lambda bi, p, t: (bi, t, p))
    keys = pl.BlockSpec((1, s, pair_w), lambda bi, p, t: (bi, 0, p))
    return pl.pallas_call(
        _mla_kernel,
        grid=(b, n_pairs, s // tq),
        in_specs=[qblk, keys, keys],
        out_specs=pl.BlockSpec((1, tq, 2 * MLA_V_DIM), lambda bi, p, t: (bi, t, p)),
        out_shape=jax.ShapeDtypeStruct((b, s, MLA_WIDTH), jnp.bfloat16),
        scratch_shapes=[pltpu.VMEM((2, tq, s), jnp.float32)],
        compiler_params=pltpu.CompilerParams(
            dimension_semantics=("arbitrary", "arbitrary", "arbitrary"),
            vmem_limit_bytes=VMEM_LIMIT),
        name="mla_attn",
    )(qm, kr, kv)


def _rope_tables(seq):
    t = jnp.arange(seq)
    row = (t // GRID_W).astype(jnp.float32)
    col = (t % GRID_W).astype(jnp.float32)
    n_freq = MLA_QK_ROPE // 4
    inv_freq = 1.0 / (ROPE_THETA ** (jnp.arange(n_freq, dtype=jnp.float32) / n_freq))
    ang = jnp.concatenate([row[:, None] * inv_freq[None, :], col[:, None] * inv_freq[None, :]], axis=-1)
    cos, sin = jnp.cos(ang), jnp.sin(ang)
    zeros = lambda width: jnp.zeros((seq, width), jnp.float32)
    tail = LANES - ROT_LANE0 - MLA_QK_ROPE
    cos_t = jnp.concatenate([jnp.ones((seq, ROT_LANE0), jnp.float32), cos, cos, zeros(tail)], axis=-1)
    sin_lo = jnp.concatenate([zeros(ROT_LANE0), -sin, zeros(HALF_ROT + tail)], axis=-1)
    sin_hi = jnp.concatenate([zeros(ROT_LANE0 + HALF_ROT), sin, zeros(tail)], axis=-1)
    return cos_t, sin_lo, sin_hi


def _pad_cols(w, left, right):
    return jnp.pad(w, ((0, 0), (left, right)))


def kernel(x, ffn1_pre_g, ffn1_w_gu, ffn1_w_down, ffn1_post_g, mix_pre_g, w_in, na_rpb, mla_q_norm_g, mla_w_uq, mla_kv_norm_g, mla_w_ukv, na_out_norm_g, mla_out_norm_g, w_out, mix_post_g, ffn2_pre_g, ffn2_w_gu, ffn2_w_down, ffn2_post_g):
    B, S, D = x.shape
    depth = ffn1_w_gu.shape[0]
    d_ff = ffn1_w_down.shape[1]
    bf = jnp.bfloat16
    n_rows = S // GRID_W
    cos_t, sin_lo, sin_hi = _rope_tables(S)
    row2 = lambda g: g.reshape(1, -1)

    def ffn_params(pre_g, w_gu, w_down, post_g):
        return (row2(pre_g), w_gu[:, :d_ff].astype(bf), w_gu[:, d_ff:].astype(bf), w_down.astype(bf),
                row2(post_g))

    h = x
    for l in range(depth):
        h = _ffn(h, ffn_params(ffn1_pre_g[l], ffn1_w_gu[l], ffn1_w_down[l], ffn1_post_g[l]))

        n_main = 3 * NA_WIDTH + MLA_Q_LORA + MLA_KV_LORA
        w_in_l = jnp.concatenate(
            [w_in[l, :, :n_main],
             _pad_cols(w_in[l, :, n_main:], ROT_LANE0, LANES - ROT_LANE0 - MLA_QK_ROPE)], axis=-1).astype(bf)
        qk = MLA_QK_NOPE + MLA_QK_ROPE
        w_uq_l = jnp.pad(mla_w_uq[l].reshape(MLA_Q_LORA, MLA_HEADS, qk),
                         ((0, 0), (0, 0), (0, HEAD_SLOT - qk))).reshape(MLA_Q_LORA, MLA_HEADS * HEAD_SLOT).astype(bf)
        qna, kna, vna, qm, kr, v1 = _proj(
            h, row2(mix_pre_g[l]), w_in_l, row2(mla_q_norm_g[l]), w_uq_l, row2(mla_kv_norm_g[l]),
            mla_w_ukv[l].astype(bf), cos_t, sin_lo, sin_hi)

        bias = _na_bias(na_rpb[l].astype(jnp.float32), n_rows)
        o_na = _na_attention(qna, kna, vna, bias)
        o_mla = _mla_attention(qm, kr, v1)

        mix = (o_na, o_mla, row2(na_out_norm_g[l]), row2(mla_out_norm_g[l]),
               w_out[l, :NA_WIDTH].astype(bf), w_out[l, NA_WIDTH:].astype(bf), row2(mix_post_g[l]))
        h = _ffn(h, ffn_params(ffn2_pre_g[l], ffn2_w_gu[l], ffn2_w_down[l], ffn2_post_g[l]), mix=mix)
    return h
```

```python
import functools

import numpy as np
import jax
import jax.numpy as jnp
from jax import lax
from jax.experimental import pallas as pl
from jax.experimental.pallas import tpu as pltpu

GRID_W = 64
EPS = 1e-6
NEG_INF = -1e30

NA_HEADS = 8
NA_HEAD_DIM = 64
NA_WIN_H = 8
NA_WIN_W = 16
NA_WIDTH = NA_HEADS * NA_HEAD_DIM

MLA_HEADS = 8
MLA_QK_NOPE = 64
MLA_QK_ROPE = 32
MLA_V_DIM = 64
MLA_Q_LORA = 256
MLA_KV_LORA = 128
MLA_WIDTH = MLA_HEADS * MLA_V_DIM
ROPE_THETA = 10000.0

MXU_WIDTH = 256
LANES = 128
HEAD_SLOT = LANES
ROT_LANE0 = MLA_QK_NOPE
HALF_ROT = MLA_QK_ROPE // 2

NA_GROUP_ROWS = 4
NA_KEY_ROWS = 2 * ((NA_GROUP_ROWS + NA_WIN_H) // 2)

VMEM_LIMIT = 56 * 1024 * 1024

_NT = (((1,), (1,)), ((), ()))
LOG2_E = float(np.log2(np.e))


def _rms(x, g):
    return x * lax.rsqrt(jnp.mean(x * x, axis=-1, keepdims=True) + EPS) * g


def _bdot(a, b):
    return jnp.dot(a, b, preferred_element_type=jnp.float32)


def _token_rows(seq, tm, width):
    tiles = seq // tm
    return pl.BlockSpec((None, tm, width), lambda i: (i // tiles, i % tiles, 0))


def _resident(a):
    return pl.BlockSpec(a.shape, lambda i: (0,) * a.ndim)


def _sub_tiles(n_rows, sub_rows):
    return [slice(r0, r0 + sub_rows) for r0 in range(0, n_rows, sub_rows)]


def _swiglu(xn, wg_ref, wu_ref, wd_ref, ff_chunk):
    d_ff = wg_ref.shape[1]
    acc = jnp.zeros((xn.shape[0], wd_ref.shape[1]), jnp.float32)
    for c0 in range(0, d_ff, ff_chunk):
        c1 = min(c0 + ff_chunk, d_ff)
        gate = _bdot(xn, wg_ref[:, c0:c1])
        up = _bdot(xn, wu_ref[:, c0:c1])
        act = (gate * jax.nn.sigmoid(gate) * up).astype(jnp.bfloat16)
        acc = acc + _bdot(act, wd_ref[c0:c1, :])
    return acc


def _ffn_half_steps(xs, pre_g_ref, wg_ref, wu_ref, wd_ref, post_g_ref, ff_chunk):
    xns = [_rms(x, pre_g_ref[...]).astype(jnp.bfloat16) for x in xs]
    accs = [_swiglu(xn, wg_ref, wu_ref, wd_ref, ff_chunk) for xn in xns]
    return [x + 0.5 * _rms(acc, post_g_ref[...]) for x, acc in zip(xs, accs)]


def _ffn_kernel(x_ref, pre_g_ref, wg_ref, wu_ref, wd_ref, post_g_ref, o_ref, *, ff_chunk, sub_rows):
    tiles = _sub_tiles(x_ref.shape[0], sub_rows)
    outs = _ffn_half_steps([x_ref[rows] for rows in tiles], pre_g_ref, wg_ref, wu_ref, wd_ref,
                           post_g_ref, ff_chunk)
    for rows, out in zip(tiles, outs):
        o_ref[rows] = out


def _mix_ffn_kernel(h_ref, ona_ref, omla_ref, na_g_ref, mla_g_ref, w_na_ref, w_mla_ref, mix_post_g_ref,
                    pre_g_ref, wg_ref, wu_ref, wd_ref, post_g_ref, o_ref, *, ff_chunk, sub_rows):
    tiles = _sub_tiles(h_ref.shape[0], sub_rows)
    hs = []
    for rows in tiles:
        na = _rms(ona_ref[rows].astype(jnp.float32), na_g_ref[...]).astype(jnp.bfloat16)
        mla = _rms(omla_ref[rows].astype(jnp.float32), mla_g_ref[...]).astype(jnp.bfloat16)
        mixed = _bdot(na, w_na_ref[...]) + _bdot(mla, w_mla_ref[...])
        hs.append(h_ref[rows] + _rms(mixed, mix_post_g_ref[...]))
    outs = _ffn_half_steps(hs, pre_g_ref, wg_ref, wu_ref, wd_ref, post_g_ref, ff_chunk)
    for rows, out in zip(tiles, outs):
        o_ref[rows] = out


def _ffn(x, ffn_params, mix=None, *, tm=1024, sub_rows=256, ff_chunk=6 * MXU_WIDTH):
    b, s, d = x.shape
    d_ff = ffn_params[1].shape[1]
    assert s % tm == 0 and tm % sub_rows == 0 and ff_chunk % MXU_WIDTH == 0 and d_ff % MXU_WIDTH == 0
    row = functools.partial(_token_rows, s, tm)
    if mix is None:
        body, operands, specs = _ffn_kernel, [x], [row(d)]
    else:
        body, operands = _mix_ffn_kernel, [x, *mix]
        specs = [row(d), row(mix[0].shape[-1]), row(mix[1].shape[-1])] + [_resident(a) for a in mix[2:]]
    return pl.pallas_call(
        functools.partial(body, ff_chunk=ff_chunk, sub_rows=sub_rows),
        grid=(b * s // tm,),
        in_specs=specs + [_resident(a) for a in ffn_params],
        out_specs=row(d),
        out_shape=jax.ShapeDtypeStruct((b, s, d), jnp.float32),
        compiler_params=pltpu.CompilerParams(
            dimension_semantics=("arbitrary",), vmem_limit_bytes=VMEM_LIMIT),
        name="ffn" if mix is None else "mix_ffn",
    )(*operands, *ffn_params)


def _rope(x, cos_t, sin_lo, sin_hi):
    from_hi = pltpu.roll(x, LANES - HALF_ROT, axis=1)
    from_lo = pltpu.roll(x, HALF_ROT, axis=1)
    return x * cos_t + from_hi * sin_lo + from_lo * sin_hi


def _proj_kernel(h_ref, pre_g_ref, w_in_ref, qn_g_ref, w_uq_ref, kvn_g_ref, w_ukv_ref,
                 cos_ref, sin_lo_ref, sin_hi_ref,
                 qna_ref, kna_ref, vna_ref, qm_ref, kr_ref, v1_ref, *, na_scale, mla_scale, sub_rows):
    w = NA_WIDTH
    tiles = _sub_tiles(h_ref.shape[0], sub_rows)
    hns = [_rms(h_ref[rows], pre_g_ref[...]).astype(jnp.bfloat16) for rows in tiles]
    z_lats = [_bdot(hn, w_in_ref[:, 3 * w:]) for hn in hns]
    latents = []
    for rows, z_lat in zip(tiles, z_lats):
        c_q = z_lat[:, :MLA_Q_LORA]
        c_kv = z_lat[:, MLA_Q_LORA:MLA_Q_LORA + MLA_KV_LORA]
        k_rope = z_lat[:, MLA_Q_LORA + MLA_KV_LORA:]
        tabs = (cos_ref[rows], sin_lo_ref[rows], sin_hi_ref[rows])
        q = _bdot(_rms(c_q, qn_g_ref[...]).astype(jnp.bfloat16), w_uq_ref[...])
        kv = _bdot(_rms(c_kv, kvn_g_ref[...]).astype(jnp.bfloat16), w_ukv_ref[...])
        latents.append((q, kv, _rope(k_rope, *tabs), tabs))

    for rows, hn in zip(tiles, hns):
        qna_ref[rows] = (_bdot(hn, w_in_ref[:, 0:w]) * na_scale).astype(jnp.bfloat16)
        kna_ref[rows] = _bdot(hn, w_in_ref[:, w:2 * w]).astype(jnp.bfloat16)
        vna_ref[rows] = _bdot(hn, w_in_ref[:, 2 * w:3 * w]).astype(jnp.bfloat16)

    lane = lax.broadcasted_iota(jnp.int32, (sub_rows, HEAD_SLOT), 1)
    nope = lane < MLA_QK_NOPE
    for rows, (q, kv, k_rot, tabs) in zip(tiles, latents):
        for h in range(MLA_HEADS):
            sl = slice(h * HEAD_SLOT, (h + 1) * HEAD_SLOT)
            qm_ref[rows, sl] = (_rope(q[:, sl], *tabs) * mla_scale).astype(jnp.bfloat16)
            kr_ref[rows, sl] = jnp.where(nope, kv[:, sl], k_rot).astype(jnp.bfloat16)
            v1_ref[rows, sl] = jnp.where(nope, 1.0, kv[:, sl]).astype(jnp.bfloat16)


def _proj(h, pre_g, w_in, qn_g, w_uq, kvn_g, w_ukv, cos_t, sin_lo, sin_hi, *, tm=1024, sub_rows=256):
    b, s, d = h.shape
    assert s % tm == 0 and tm % sub_rows == 0
    tiles_per_seq = s // tm
    row = functools.partial(_token_rows, s, tm)
    full = _resident
    tab = pl.BlockSpec((tm, LANES), lambda i: (i % tiles_per_seq, 0))
    slots = MLA_HEADS * HEAD_SLOT
    bf = jnp.bfloat16
    return pl.pallas_call(
        functools.partial(_proj_kernel, na_scale=NA_HEAD_DIM ** -0.5 * LOG2_E,
                          mla_scale=(MLA_QK_NOPE + MLA_QK_ROPE) ** -0.5 * LOG2_E, sub_rows=sub_rows),
        grid=(b * s // tm,),
        in_specs=[row(d), full(pre_g), full(w_in), full(qn_g), full(w_uq), full(kvn_g),
                  full(w_ukv), tab, tab, tab],
        out_specs=[row(NA_WIDTH), row(NA_WIDTH), row(NA_WIDTH), row(slots), row(slots), row(slots)],
        out_shape=[jax.ShapeDtypeStruct((b, s, NA_WIDTH), bf)] * 3
        + [jax.ShapeDtypeStruct((b, s, slots), bf)] * 3,
        compiler_params=pltpu.CompilerParams(
            dimension_semantics=("arbitrary",), vmem_limit_bytes=VMEM_LIMIT),
        name="proj",
    )(h, pre_g, w_in, qn_g, w_uq, kvn_g, w_ukv, cos_t, sin_lo, sin_hi)


def _na_kernel(q_ref, k_ref, v_ref, bias_a_ref, bias_b_ref, o_ref, *, n_rows):
    for u, bias_ref in enumerate((bias_a_ref, bias_b_ref)):
        rows = slice(u * NA_GROUP_ROWS * GRID_W, (u + 1) * NA_GROUP_ROWS * GRID_W)
        _na_group(q_ref, k_ref, v_ref, bias_ref, o_ref, rows, 2 * pl.program_id(1) + u, n_rows)


def _na_group(q_ref, k_ref, v_ref, bias_ref, o_ref, rows, g, n_rows):
    key_row0 = jnp.clip(g * NA_GROUP_ROWS - NA_WIN_H // 2, 0, n_rows - NA_KEY_ROWS)
    start = pl.multiple_of(key_row0 * GRID_W, GRID_W)
    n_keys = NA_KEY_ROWS * GRID_W
    lane = lax.broadcasted_iota(jnp.int32, (NA_GROUP_ROWS * GRID_W, LANES), 1)
    low = lane < NA_HEAD_DIM
    ones = jnp.ones((n_keys, LANES), jnp.bfloat16)
    for pair in range(NA_HEADS // 2):
        sl = slice(pair * LANES, (pair + 1) * LANES)
        q = q_ref[0, rows, sl]
        k = k_ref[0, pl.ds(start, n_keys), sl]
        v1 = jnp.concatenate([v_ref[0, pl.ds(start, n_keys), sl], ones], axis=1)
        outs = []
        for e in range(2):
            qe = jnp.where(low if e == 0 else ~low, q, jnp.zeros_like(q))
            s = lax.dot_general(qe, k, _NT, preferred_element_type=jnp.float32)
            s = s + jnp.concatenate(
                [jnp.concatenate([bias_ref[0, qr, j, 2 * pair + e] for j in range(NA_KEY_ROWS // 2)], axis=1)
                 for qr in range(NA_GROUP_ROWS)], axis=0)
            p = jnp.exp2(s - jnp.max(s, axis=-1, keepdims=True))
            o = _bdot(p.astype(jnp.bfloat16), v1)
            outs.append(o[:, :LANES] / o[:, LANES:])
        o_ref[0, rows, sl] = jnp.where(low, outs[0], outs[1]).astype(o_ref.dtype)


def _na_bias(rpb, n_rows):
    G, KR, W = NA_GROUP_ROWS, NA_KEY_ROWS, GRID_W
    n_groups = n_rows // G
    n_pairs = KR // 2
    kh = min(NA_WIN_H, n_rows)
    n_ro, n_co = 2 * NA_WIN_H - 1, 2 * NA_WIN_W - 1
    qc = np.arange(W)[:, None]
    kc = np.arange(W)[None, :]
    win0 = np.clip(qc - NA_WIN_W // 2, 0, W - NA_WIN_W)
    col_ok = (kc >= win0) & (kc < win0 + NA_WIN_W)
    col_off = np.clip(kc - qc + NA_WIN_W - 1, 0, n_co - 1)
    col_sel = np.zeros((2, n_co, W, 2, W), np.float32)
    for half in range(2):
        col_sel[half, :, :, half, :] = np.eye(n_co, dtype=np.float32)[:, col_off]
    col_sel = jnp.asarray(col_sel.reshape(2 * n_co, W, 2 * W))
    rpb_pad = jnp.pad(rpb, ((0, 0), (1, 1), (0, 0)))
    first_second = jnp.concatenate([rpb_pad[:, :-1], rpb_pad[:, 1:]], axis=-1)
    pair_tab = jnp.einsum('hik,kql->hiql', first_second, col_sel, precision=lax.Precision.HIGHEST)
    col_ok2 = np.tile(col_ok, (1, 2))
    pair_tab = jnp.where(col_ok2, pair_tab * LOG2_E, NEG_INF)
    half_masks = np.zeros((2, W, 2 * W), np.float32)
    half_masks[0, :, :W] = NEG_INF
    half_masks[1, :, W:] = NEG_INF
    table = jnp.concatenate(
        [pair_tab, jnp.broadcast_to(jnp.asarray(half_masks), (rpb.shape[0],) + half_masks.shape)], axis=1)

    sels = []
    for g in (0, 1, n_groups - 1):
        r = g * G + np.arange(G)[:, None]
        key_row0 = np.clip(g * G - NA_WIN_H // 2, 0, n_rows - KR)
        kr = key_row0 + np.arange(KR)[None, :]
        row0 = np.clip(r - kh // 2, 0, n_rows - kh)
        row_bad = ~((kr >= row0) & (kr < row0 + kh))
        first_off = np.clip(kr[:, 0::2] - r + NA_WIN_H - 1, -1, n_ro - 1)
        pick = np.eye(n_ro + 1, dtype=np.float32)[first_off + 1]
        flags = np.stack([row_bad[:, 0::2], row_bad[:, 1::2]], axis=-1).astype(np.float32)
        sels.append(np.concatenate([pick, flags], axis=-1))
    sel = jnp.asarray(np.stack(sels))
    return jnp.einsum('vgji,hiql->vgjhql', sel, table, precision=lax.Precision.HIGHEST)


def _na_attention(q, k, v, bias):
    b, s, w = q.shape
    n_rows = s // GRID_W
    assert n_rows % (2 * NA_GROUP_ROWS) == 0 and n_rows >= NA_KEY_ROWS and n_rows // NA_GROUP_ROWS >= 4
    n_steps = n_rows // (2 * NA_GROUP_ROWS)
    tq = 2 * NA_GROUP_ROWS * GRID_W
    bias_block = (1,) + bias.shape[1:]
    bias_a = pl.BlockSpec(bias_block, lambda bi, g: (jnp.where(g == 0, 0, 1), 0, 0, 0, 0, 0))
    bias_b = pl.BlockSpec(bias_block, lambda bi, g: (jnp.where(g == n_steps - 1, 2, 1), 0, 0, 0, 0, 0))

    whole = pl.BlockSpec((1, s, w), lambda bi, g: (bi, 0, 0))
    qblk = pl.BlockSpec((1, tq, w), lambda bi, g: (bi, g, 0))
    return pl.pallas_call(
        functools.partial(_na_kernel, n_rows=n_rows),
        grid=(b, n_steps),
        in_specs=[qblk, whole, whole, bias_a, bias_b],
        out_specs=qblk,
        out_shape=jax.ShapeDtypeStruct((b, s, w), jnp.bfloat16),
        compiler_params=pltpu.CompilerParams(
            dimension_semantics=("arbitrary", "arbitrary"), vmem_limit_bytes=VMEM_LIMIT),
        name="na_attn",
    )(q, k, v, bias, bias)


def _mla_kernel(q_ref, kr_ref, kv_ref, o_ref, s_ref):
    tq = q_ref.shape[1]
    slots = [slice(e * HEAD_SLOT, (e + 1) * HEAD_SLOT) for e in range(2)]
    m = []
    for e, sl in enumerate(slots):
        s = lax.dot_general(q_ref[0, :, sl], kr_ref[0, :, sl], _NT, preferred_element_type=jnp.float32)
        s_ref[e] = s
        m.append(jnp.max(s, axis=-1, keepdims=True))
    acc = []
    for e, sl in enumerate(slots):
        p = jnp.exp2(s_ref[e] - m[e])
        acc.append(_bdot(p.astype(jnp.bfloat16), kv_ref[0, :, sl]))

    lane = lax.broadcasted_iota(jnp.int32, (tq, HEAD_SLOT), 1)
    swapped = [pltpu.roll(a, LANES - MLA_V_DIM, axis=1) for a in acc]
    o_ref[0] = jnp.where(lane < MLA_V_DIM, swapped[0] / acc[0], acc[1] / swapped[1]).astype(o_ref.dtype)


def _mla_attention(qm, kr, kv, *, tq=1024):
    b, s, slots = qm.shape
    n_pairs = MLA_HEADS // 2
    assert s % tq == 0 and slots == MLA_HEADS * HEAD_SLOT
    pair_w = 2 * HEAD_SLOT
    qblk = pl.BlockSpec((1, tq, pair_w), lambda bi, p, t: (bi, t, p))
    keys = pl.BlockSpec((1, s, pair_w), lambda bi, p, t: (bi, 0, p))
    return pl.pallas_call(
        _mla_kernel,
        grid=(b, n_pairs, s // tq),
        in_specs=[qblk, keys, keys],
        out_specs=pl.BlockSpec((1, tq, 2 * MLA_V_DIM), lambda bi, p, t: (bi, t, p)),
        out_shape=jax.ShapeDtypeStruct((b, s, MLA_WIDTH), jnp.bfloat16),
        scratch_shapes=[pltpu.VMEM((2, tq, s), jnp.float32)],
        compiler_params=pltpu.CompilerParams(
            dimension_semantics=("arbitrary", "arbitrary", "arbitrary"),
            vmem_limit_bytes=VMEM_LIMIT),
        name="mla_attn",
    )(qm, kr, kv)


def _rope_tables(seq):
    t = jnp.arange(seq)
    row = (t // GRID_W).astype(jnp.float32)
    col = (t % GRID_W).astype(jnp.float32)
    n_freq = MLA_QK_ROPE // 4
    inv_freq = 1.0 / (ROPE_THETA ** (jnp.arange(n_freq, dtype=jnp.float32) / n_freq))
    ang = jnp.concatenate([row[:, None] * inv_freq[None, :], col[:, None] * inv_freq[None, :]], axis=-1)
    cos, sin = jnp.cos(ang), jnp.sin(ang)
    zeros = lambda width: jnp.zeros((seq, width), jnp.float32)
    tail = LANES - ROT_LANE0 - MLA_QK_ROPE
    cos_t = jnp.concatenate([jnp.ones((seq, ROT_LANE0), jnp.float32), cos, cos, zeros(tail)], axis=-1)
    sin_lo = jnp.concatenate([zeros(ROT_LANE0), -sin, zeros(HALF_ROT + tail)], axis=-1)
    sin_hi = jnp.concatenate([zeros(ROT_LANE0 + HALF_ROT), sin, zeros(tail)], axis=-1)
    return cos_t, sin_lo, sin_hi


def _pad_cols(w, left, right):
    return jnp.pad(w, ((0, 0), (left, right)))


def kernel(x, ffn1_pre_g, ffn1_w_gu, ffn1_w_down, ffn1_post_g, mix_pre_g, w_in, na_rpb, mla_q_norm_g, mla_w_uq, mla_kv_norm_g, mla_w_ukv, na_out_norm_g, mla_out_norm_g, w_out, mix_post_g, ffn2_pre_g, ffn2_w_gu, ffn2_w_down, ffn2_post_g):
    B, S, D = x.shape
    depth = ffn1_w_gu.shape[0]
    d_ff = ffn1_w_down.shape[1]
    bf = jnp.bfloat16
    n_rows = S // GRID_W
    cos_t, sin_lo, sin_hi = _rope_tables(S)
    row2 = lambda g: g.reshape(1, -1)

    def ffn_params(pre_g, w_gu, w_down, post_g):
        return (row2(pre_g), w_gu[:, :d_ff].astype(bf), w_gu[:, d_ff:].astype(bf), w_down.astype(bf),
                row2(post_g))

    h = x
    for l in range(depth):
        h = _ffn(h, ffn_params(ffn1_pre_g[l], ffn1_w_gu[l], ffn1_w_down[l], ffn1_post_g[l]))

        n_main = 3 * NA_WIDTH + MLA_Q_LORA + MLA_KV_LORA
        w_in_l = jnp.concatenate(
            [w_in[l, :, :n_main],
             _pad_cols(w_in[l, :, n_main:], ROT_LANE0, LANES - ROT_LANE0 - MLA_QK_ROPE)], axis=-1).astype(bf)
        qk = MLA_QK_NOPE + MLA_QK_ROPE
        w_uq_l = jnp.pad(mla_w_uq[l].reshape(MLA_Q_LORA, MLA_HEADS, qk),
                         ((0, 0), (0, 0), (0, HEAD_SLOT - qk))).reshape(MLA_Q_LORA, MLA_HEADS * HEAD_SLOT).astype(bf)
        qna, kna, vna, qm, kr, v1 = _proj(
            h, row2(mix_pre_g[l]), w_in_l, row2(mla_q_norm_g[l]), w_uq_l, row2(mla_kv_norm_g[l]),
            mla_w_ukv[l].astype(bf), cos_t, sin_lo, sin_hi)

        bias = _na_bias(na_rpb[l].astype(jnp.float32), n_rows)
        o_na = _na_attention(qna, kna, vna, bias)
        o_mla = _mla_attention(qm, kr, v1)

        mix = (o_na, o_mla, row2(na_out_norm_g[l]), row2(mla_out_norm_g[l]),
               w_out[l, :NA_WIDTH].astype(bf), w_out[l, NA_WIDTH:].astype(bf), row2(mix_post_g[l]))
        h = _ffn(h, ffn_params(ffn2_pre_g[l], ffn2_w_gu[l], ffn2_w_down[l], ffn2_post_g[l]), mix=mix)
    return h
```
